```python
import math
import jax, jax.numpy as jnp
from jax import lax
import numpy as np

D_MODEL = 2048
BATCH = 4
SEQ = 8192
DEPTH = 4
DEC_BATCH = 8
DEC_SEQ = 32
PAST_LEN = 1024

CHUNK = 64
N_MIXERS = 4
N_A = (DEPTH + 3) // 4
N_B = (DEPTH + 2) // 4
N_C = (DEPTH + 1) // 4
N_D = DEPTH // 4

A_BLOCK = 128
A_GROUPS = 8
A_GROUP_DIM = D_MODEL // A_GROUPS
B_HEADS = 8
B_DV = D_MODEL // B_HEADS
B_DQK = B_DV // 2
C_HEADS = 8
C_DK = D_MODEL // C_HEADS
C_DV = 2 * C_DK
ROPE_BASE = 10000.0
D_HEADS = 16
D_HEAD_DIM = D_MODEL // D_HEADS
Q_BLOCK = 128
FFN_HIDDEN = ((8 * D_MODEL + 3 * 256 - 1) // (3 * 256)) * 256
ALPHA = (2.0 * DEPTH) ** 0.25
BETA = (8.0 * DEPTH) ** -0.25
LN_EPS = 1e-5

kernel_name = 'hybrid_streaming_encoder_step'


def layer_norm(x, g, b):
    xf = x.astype(jnp.float32)
    mu = jnp.mean(xf, axis=-1, keepdims=True)
    var = jnp.mean(jnp.square(xf - mu), axis=-1, keepdims=True)
    y = (xf - mu) * lax.rsqrt(var + LN_EPS)
    return (y * g.astype(jnp.float32) + b.astype(jnp.float32)).astype(x.dtype)


def head_rms_norm(h, g):
    y = h * lax.rsqrt(jnp.mean(h * h, axis=-1, keepdims=True) + LN_EPS)
    return y.reshape(*h.shape[:-2], -1) * g.astype(jnp.float32)


def head_layer_norm(h, g, b):
    mu = jnp.mean(h, axis=-1, keepdims=True)
    var = jnp.mean(jnp.square(h - mu), axis=-1, keepdims=True)
    y = ((h - mu) * lax.rsqrt(var + LN_EPS)).reshape(*h.shape[:-2], -1)
    return y * g.astype(jnp.float32) + b.astype(jnp.float32)


def rotary(x, pos):
    half = x.shape[-1] // 2
    inv = ROPE_BASE ** (-jnp.arange(half, dtype=jnp.float32) / half)
    ang = pos.astype(jnp.float32)[:, None] * inv[None, :]
    cos = jnp.cos(ang)[None, :, None, :]
    sin = jnp.sin(ang)[None, :, None, :]
    x1, x2 = x[..., :half], x[..., half:]
    return jnp.concatenate([x1 * cos - x2 * sin, x1 * sin + x2 * cos], axis=-1)


def to_chunks(a, block):
    b, t = a.shape[:2]
    return jnp.moveaxis(a.reshape(b, t // block, block, *a.shape[2:]), 1, 0)


def from_chunks(a):
    nc, b, l = a.shape[:3]
    return jnp.moveaxis(a, 0, 1).reshape(b, nc * l, *a.shape[3:])


def post_norm(x, sub, g, b):
    return layer_norm(ALPHA * x + sub, g, b)


def swiglu(x, w_in, w_out):
    gate, up = jnp.split(x @ w_in, 2, axis=-1)
    return (jax.nn.silu(gate) * up) @ w_out


def gmlp_mixer(x, w_in, b_in, vn_g, vn_b, w_s, b_s, w_out, block):
    bsz, t, _ = x.shape
    z = jax.nn.gelu(x @ w_in + b_in)
    u, v = jnp.split(z, 2, axis=-1)
    v = layer_norm(v, vn_g, vn_b)
    pos = jnp.arange(block)
    mask = (pos[None, :] // CHUNK) <= (pos[:, None] // CHUNK)
    ws = jnp.where(mask[None], w_s[:, :block, :block], 0.0).astype(x.dtype)
    vb = v.reshape(bsz, t // block, block, A_GROUPS, A_GROUP_DIM)
    s = jnp.einsum('gpq,bcqge->bcpge', ws, vb) + jnp.transpose(b_s[:, :block])[None, None, :, :, None]
    y = u * s.reshape(bsz, t, D_MODEL)
    return y @ w_out, v


def mlstm_mixer(x, w_in, b_gates, norm_g, w_out, c0, n0, m0, block):
    f32 = jnp.float32
    bsz, t, _ = x.shape
    hq = B_HEADS * B_DQK
    q, k, v, og, gates = jnp.split(x @ w_in, [hq, 2 * hq, 2 * hq + D_MODEL, 2 * hq + 2 * D_MODEL], axis=-1)
    q = q.reshape(bsz, t, B_HEADS, B_DQK).astype(f32) * (B_DQK ** -0.5)
    k = k.reshape(bsz, t, B_HEADS, B_DQK).astype(f32)
    v = v.reshape(bsz, t, B_HEADS, B_DV).astype(f32)
    gates = gates.astype(f32) + b_gates.astype(f32)
    log_i, f_pre = jnp.split(gates, 2, axis=-1)
    log_f = jax.nn.log_sigmoid(f_pre)
    xs = tuple(to_chunks(a, block) for a in (q, k, v, log_i, log_f))
    causal = jnp.tril(jnp.ones((block, block), dtype=bool))

    def step(carry, inp):
        c, n, m = carry
        qc, kc, vc, ic, fc = inp
        bcum = jnp.cumsum(fc, axis=1)
        dmat = bcum[:, :, None, :] - bcum[:, None, :, :] + ic[:, None, :, :]
        dmat = jnp.where(causal[None, :, :, None], dmat, -jnp.inf)
        inter = bcum + m[:, None, :]
        m_t = jnp.maximum(inter, jnp.max(dmat, axis=2))
        w_intra = jnp.exp(dmat - m_t[:, :, None, :])
        w_inter = jnp.exp(inter - m_t)
        scores = jnp.einsum('bthd,bshd->btsh', qc, kc) * w_intra
        num = jnp.einsum('btsh,bshe->bthe', scores, vc) + w_inter[..., None] * jnp.einsum('bhed,bthd->bthe', c, qc)
        den = jnp.sum(scores, axis=2) + w_inter * jnp.einsum('bhd,bthd->bth', n, qc)
        h = num / jnp.maximum(jnp.abs(den), jnp.exp(-m_t))[..., None]
        m_new = m_t[:, -1]
        b_last = bcum[:, -1]
        w_c = jnp.exp(b_last + m - m_new)
        w_s = jnp.exp(b_last[:, None, :] - bcum + ic - m_new[:, None, :])
        c_new = w_c[..., None, None] * c + jnp.einsum('bsh,bshe,bshd->bhed', w_s, vc, kc)
        n_new = w_c[..., None] * n + jnp.einsum('bsh,bshd->bhd', w_s, kc)
        return (c_new, n_new, m_new), h

    (c, n, m), hs = lax.scan(step, (c0.astype(f32), n0.astype(f32), m0.astype(f32)), xs)
    h = from_chunks(hs)
    h = head_rms_norm(h, norm_g) * jax.nn.sigmoid(og.astype(f32))
    return h.astype(x.dtype) @ w_out, c, n, m


def retention_mixer(x, w_in, gn_g, gn_b, w_out, s0, pos0, block):
    f32 = jnp.float32
    bsz, t, _ = x.shape
    hk = C_HEADS * C_DK
    q, k, v, g = jnp.split(x @ w_in, [hk, 2 * hk, 2 * hk + C_HEADS * C_DV], axis=-1)
    pos = pos0 + jnp.arange(t)
    q = rotary(q.reshape(bsz, t, C_HEADS, C_DK).astype(f32), pos)
    k = rotary(k.reshape(bsz, t, C_HEADS, C_DK).astype(f32), pos) * (C_DK ** -0.5)
    v = v.reshape(bsz, t, C_HEADS, C_DV).astype(f32)
    log_gamma = jnp.log1p(-(2.0 ** (-5.0 - jnp.arange(C_HEADS, dtype=f32))))
    idx = jnp.arange(block, dtype=f32)
    causal = idx[:, None] >= idx[None, :]
    decay_intra = jnp.where(causal[:, :, None], jnp.exp((idx[:, None] - idx[None, :])[:, :, None] * log_gamma), 0.0)
    decay_q = jnp.exp((idx + 1.0)[:, None] * log_gamma)
    decay_k = jnp.exp((block - 1.0 - idx)[:, None] * log_gamma)
    decay_s = jnp.exp(block * log_gamma)
    xs = tuple(to_chunks(a, block) for a in (q, k, v))

    def step(s, inp):
        qc, kc, vc = inp
        scores = jnp.einsum('bthd,bshd->btsh', qc, kc) * decay_intra
        o = jnp.einsum('btsh,bshe->bthe', scores, vc) + jnp.einsum('bthd,bhde->bthe', qc, s) * decay_q[None, :, :, None]
        s_new = decay_s[None, :, None, None] * s + jnp.einsum('bshd,sh,bshe->bhde', kc, decay_k, vc)
        return s_new, o

    s_fin, outs = lax.scan(step, s0.astype(f32), xs)
    o = head_layer_norm(from_chunks(outs), gn_g, gn_b)
    y = (jax.nn.silu(g.astype(f32)) * o).astype(x.dtype) @ w_out
    return y, s_fin


def fox_project(x, w_in, b_f):
    f32 = jnp.float32
    bsz, t, _ = x.shape
    q, k, v, f_pre = jnp.split(x @ w_in, [D_MODEL, 2 * D_MODEL, 3 * D_MODEL], axis=-1)
    q = q.reshape(bsz, t, D_HEADS, D_HEAD_DIM).astype(f32) * (D_HEAD_DIM ** -0.5)
    k = k.reshape(bsz, t, D_HEADS, D_HEAD_DIM).astype(f32)
    v = v.reshape(bsz, t, D_HEADS, D_HEAD_DIM).astype(f32)
    logf = jax.nn.log_sigmoid(f_pre.astype(f32) + b_f.astype(f32))
    return q, k, v, logf


def fox_prompt(x, w_in, b_f, w_out):
    bsz, t, _ = x.shape
    q, k, v, logf = fox_project(x, w_in, b_f)
    c_t = jnp.transpose(jnp.cumsum(logf, axis=1), (0, 2, 1))
    kpos = jnp.arange(t)

    def one_block(blk):
        start = blk * Q_BLOCK
        qb = lax.dynamic_slice_in_dim(q, start, Q_BLOCK, axis=1)
        cq = lax.dynamic_slice_in_dim(c_t, start, Q_BLOCK, axis=2)
        qpos = start + jnp.arange(Q_BLOCK)
        logits = jnp.einsum('bqhd,bkhd->bhqk', qb, k) + cq[..., :, None] - c_t[..., None, :]
        logits = jnp.where(kpos[None, :] <= qpos[:, None], logits, -jnp.inf)
        p = jax.nn.softmax(logits, axis=-1)
        return jnp.einsum('bhqk,bkhd->bqhd', p, v)

    o = lax.map(one_block, jnp.arange(t // Q_BLOCK))
    o = jnp.moveaxis(o, 0, 1).reshape(bsz, t, D_MODEL)
    return o.astype(x.dtype) @ w_out, k, v, logf


def fox_sample(x, w_in, b_f, w_out, k_cache, v_cache, logf_cache):
    f32 = jnp.float32
    bsz, t, _ = x.shape
    past = k_cache.shape[1]
    q, k, v, logf = fox_project(x, w_in, b_f)
    k_all = jnp.concatenate([k_cache.astype(f32), k], axis=1)
    v_all = jnp.concatenate([v_cache.astype(f32), v], axis=1)
    c_all = jnp.transpose(jnp.cumsum(jnp.concatenate([logf_cache.astype(f32), logf], axis=1), axis=1), (0, 2, 1))
    logits = jnp.einsum('bqhd,bkhd->bhqk', q, k_all) + c_all[..., past:, None] - c_all[..., None, :]
    kpos = jnp.arange(past + t)
    qpos = past + jnp.arange(t)
    logits = jnp.where(kpos[None, :] <= qpos[:, None], logits, -jnp.inf)
    p = jax.nn.softmax(logits, axis=-1)
    o = jnp.einsum('bhqk,bkhd->bqhd', p, v_all).reshape(bsz, t, D_MODEL)
    return o.astype(x.dtype) @ w_out, k, v, logf


def setup_inputs(seed: int = 0) -> dict:
    key = jax.random.key(seed)
    ks = iter(jax.random.split(key, 48))
    f32 = jnp.float32
    D = D_MODEL

    def nrm(shape, scale):
        return jax.random.normal(next(ks), shape, f32) * scale

    inputs = {}
    inputs['x_prompt'] = nrm((BATCH, SEQ, D), 1.0)
    inputs['x_sample'] = nrm((DEC_BATCH, DEC_SEQ, D), 1.0)
    inputs['state_b_C'] = nrm((N_B, DEC_BATCH, B_HEADS, B_DV, B_DQK), 0.5)
    inputs['state_b_n'] = nrm((N_B, DEC_BATCH, B_HEADS, B_DQK), 0.5)
    inputs['state_b_m'] = nrm((N_B, DEC_BATCH, B_HEADS), 1.0)
    inputs['state_c_S'] = nrm((N_C, DEC_BATCH, C_HEADS, C_DK, C_DV), 1.0)
    inputs['cache_d_k'] = nrm((N_D, DEC_BATCH, PAST_LEN, D_HEADS, D_HEAD_DIM), 1.0)
    inputs['cache_d_v'] = nrm((N_D, DEC_BATCH, PAST_LEN, D_HEADS, D_HEAD_DIM), 1.0)
    inputs['cache_d_logf'] = jax.nn.log_sigmoid(nrm((N_D, DEC_BATCH, PAST_LEN, D_HEADS), 1.0) + 3.0)
    inputs['a_w_in'] = nrm((N_A, D, 2 * D), D ** -0.5)
    inputs['a_b_in'] = nrm((N_A, 2 * D), 0.02)
    inputs['a_vn_g'] = 1.0 + nrm((N_A, D), 0.02)
    inputs['a_vn_b'] = nrm((N_A, D), 0.02)
    inputs['a_w_s'] = nrm((N_A, A_GROUPS, A_BLOCK, A_BLOCK), A_BLOCK ** -0.5)
    inputs['a_b_s'] = 1.0 + nrm((N_A, A_GROUPS, A_BLOCK), 0.02)
    inputs['a_w_out'] = nrm((N_A, D, D), BETA * D ** -0.5)
    inputs['b_w_in'] = nrm((N_B, D, 2 * B_HEADS * B_DQK + 2 * D + 2 * B_HEADS), D ** -0.5)
    inputs['b_b_gates'] = jnp.concatenate([nrm((N_B, B_HEADS), 0.01), jnp.linspace(3.0, 6.0, B_HEADS, dtype=f32)[None, :] + nrm((N_B, B_HEADS), 0.01)], axis=-1)
    inputs['b_norm_g'] = 1.0 + nrm((N_B, D), 0.02)
    inputs['b_w_out'] = nrm((N_B, D, D), BETA * D ** -0.5)
    inputs['c_w_in'] = nrm((N_C, D, 2 * C_HEADS * C_DK + 2 * C_HEADS * C_DV), D ** -0.5)
    inputs['c_gn_g'] = 1.0 + nrm((N_C, C_HEADS * C_DV), 0.02)
    inputs['c_gn_b'] = nrm((N_C, C_HEADS * C_DV), 0.02)
    inputs['c_w_out'] = nrm((N_C, C_HEADS * C_DV, D), BETA * (C_HEADS * C_DV) ** -0.5)
    inputs['d_w_in'] = nrm((N_D, D, 3 * D + D_HEADS), D ** -0.5)
    inputs['d_b_f'] = jnp.linspace(1.0, 4.0, D_HEADS, dtype=f32)[None, :] + nrm((N_D, D_HEADS), 0.01)
    inputs['d_w_out'] = nrm((N_D, D, D), BETA * D ** -0.5)
    inputs['ffn_w_in'] = nrm((DEPTH, D, 2 * FFN_HIDDEN), D ** -0.5)
    inputs['ffn_w_out'] = nrm((DEPTH, FFN_HIDDEN, D), BETA * FFN_HIDDEN ** -0.5)
    inputs['ln1_g'] = 1.0 + nrm((DEPTH, D), 0.02)
    inputs['ln1_b'] = nrm((DEPTH, D), 0.02)
    inputs['ln2_g'] = 1.0 + nrm((DEPTH, D), 0.02)
    inputs['ln2_b'] = nrm((DEPTH, D), 0.02)
    return inputs


def reference(x_prompt, x_sample, state_b_C, state_b_n, state_b_m, state_c_S, cache_d_k, cache_d_v, cache_d_logf,
              a_w_in, a_b_in, a_vn_g, a_vn_b, a_w_s, a_b_s, a_w_out,
              b_w_in, b_b_gates, b_norm_g, b_w_out,
              c_w_in, c_gn_g, c_gn_b, c_w_out,
              d_w_in, d_b_f, d_w_out,
              ffn_w_in, ffn_w_out, ln1_g, ln1_b, ln2_g, ln2_b):
    f32 = jnp.float32
    xp, xs = x_prompt, x_sample
    bp, tp = xp.shape[0], xp.shape[1]
    ts = xs.shape[1]
    a_vs = []
    b_cp, b_np, b_mp, b_cs, b_ns, b_ms = [], [], [], [], [], []
    c_sp, c_ss = [], []
    d_kp, d_vp, d_fp, d_ks, d_vs, d_fs = [], [], [], [], [], []
    for i in range(DEPTH):
        kind, j = i % N_MIXERS, i // N_MIXERS
        if kind == 0:
            mp, _ = gmlp_mixer(xp, a_w_in[j], a_b_in[j], a_vn_g[j], a_vn_b[j], a_w_s[j], a_b_s[j], a_w_out[j], A_BLOCK)
            ms, v_rows = gmlp_mixer(xs, a_w_in[j], a_b_in[j], a_vn_g[j], a_vn_b[j], a_w_s[j], a_b_s[j], a_w_out[j], ts)
            a_vs.append(v_rows)
        elif kind == 1:
            zc = jnp.zeros((bp, B_HEADS, B_DV, B_DQK), f32)
            zn = jnp.zeros((bp, B_HEADS, B_DQK), f32)
            zm = jnp.zeros((bp, B_HEADS), f32)
            mp, cp, np_, mp_state = mlstm_mixer(xp, b_w_in[j], b_b_gates[j], b_norm_g[j], b_w_out[j], zc, zn, zm, CHUNK)
            ms, cs, ns, ms_state = mlstm_mixer(xs, b_w_in[j], b_b_gates[j], b_norm_g[j], b_w_out[j], state_b_C[j], state_b_n[j], state_b_m[j], ts)
            b_cp.append(cp); b_np.append(np_); b_mp.append(mp_state)
            b_cs.append(cs); b_ns.append(ns); b_ms.append(ms_state)
        elif kind == 2:
            zs = jnp.zeros((bp, C_HEADS, C_DK, C_DV), f32)
            mp, sp = retention_mixer(xp, c_w_in[j], c_gn_g[j], c_gn_b[j], c_w_out[j], zs, 0, CHUNK)
            ms, ss = retention_mixer(xs, c_w_in[j], c_gn_g[j], c_gn_b[j], c_w_out[j], state_c_S[j], PAST_LEN, ts)
            c_sp.append(sp); c_ss.append(ss)
        else:
            mp, kp, vp, fp = fox_prompt(xp, d_w_in[j], d_b_f[j], d_w_out[j])
            ms, kn, vn, fn = fox_sample(xs, d_w_in[j], d_b_f[j], d_w_out[j], cache_d_k[j], cache_d_v[j], cache_d_logf[j])
            d_kp.append(kp); d_vp.append(vp); d_fp.append(fp)
            d_ks.append(kn); d_vs.append(vn); d_fs.append(fn)
        xp = post_norm(xp, mp, ln1_g[i], ln1_b[i])
        xs = post_norm(xs, ms, ln1_g[i], ln1_b[i])
        xp = post_norm(xp, swiglu(xp, ffn_w_in[i], ffn_w_out[i]), ln2_g[i], ln2_b[i])
        xs = post_norm(xs, swiglu(xs, ffn_w_in[i], ffn_w_out[i]), ln2_g[i], ln2_b[i])
    return (xp, xs, jnp.stack(a_vs),
            jnp.stack(b_cp), jnp.stack(b_np), jnp.stack(b_mp),
            jnp.stack(b_cs), jnp.stack(b_ns), jnp.stack(b_ms),
            jnp.stack(c_sp), jnp.stack(c_ss),
            jnp.stack(d_kp), jnp.stack(d_vp), jnp.stack(d_fp),
            jnp.stack(d_ks), jnp.stack(d_vs), jnp.stack(d_fs))
```

```python
import functools
import math

import jax
import jax.numpy as jnp
from jax import lax
from jax.experimental import pallas as pl
from jax.experimental.pallas import tpu as pltpu

F32 = jnp.float32
BF16 = jnp.bfloat16

LN_EPS = 1e-5
ROPE_BASE = 10000.0
GMLP_BLOCK = 128
GMLP_CHUNK = 64
GMLP_GROUPS = 8
MLSTM_HEADS = 8
RET_HEADS = 8
FOX_HEADS = 16
ROW_TILE = 256
SEQ_CHUNK = 256
ATTN_BLOCK = 512
VMEM_LIMIT = 56 * 2 ** 20

NT = (((1,), (1,)), ((), ()))
TN = (((0,), (0,)), ((), ()))


def _params(*sem):
    return pltpu.CompilerParams(dimension_semantics=sem, vmem_limit_bytes=VMEM_LIMIT)


def _row_tile(m, candidates):
    for t in candidates:
        if m % t == 0:
            return t
    raise ValueError(f"no row tile for {m}")


def _dot(a, b):
    return jnp.dot(a, b, preferred_element_type=F32)


def _dg(a, b, dims):
    return lax.dot_general(a, b, dims, preferred_element_type=F32)


def _split3(x):
    hi = x.astype(BF16)
    r1 = x - hi.astype(F32)
    mid = r1.astype(BF16)
    lo = (r1 - mid.astype(F32)).astype(BF16)
    return hi, mid, lo


def _dot_exact_lhs(mat_bf16, x):
    hi, mid, lo = _split3(x)
    return _dot(mat_bf16, hi) + _dot(mat_bf16, mid) + _dot(mat_bf16, lo)


def _transpose_exact(x):
    n = x.shape[1]
    eye = (lax.broadcasted_iota(jnp.int32, (n, n), 0) == lax.broadcasted_iota(jnp.int32, (n, n), 1)).astype(BF16)
    hi, mid, lo = _split3(x)
    return _dg(eye, hi, NT) + _dg(eye, mid, NT) + _dg(eye, lo, NT)


def _sigmoid(x):
    return 1.0 / (1.0 + jnp.exp(-x))


def _log_sigmoid(x):
    return jnp.minimum(x, 0.0) - jnp.log1p(jnp.exp(-jnp.abs(x)))


def _gelu_tanh(x):
    return 0.5 * x * (1.0 + jnp.tanh(math.sqrt(2.0 / math.pi) * (x + 0.044715 * (x * x * x))))


def _layer_norm(y, g, b):
    mu = jnp.mean(y, axis=-1, keepdims=True)
    d = y - mu
    var = jnp.mean(d * d, axis=-1, keepdims=True)
    return d * lax.rsqrt(var + LN_EPS) * g + b


def _gelu_proj_kernel(x_ref, w_ref, b_ref, o_ref):
    o_ref[...] = _gelu_tanh(_dot(x_ref[...], w_ref[...]) + b_ref[...]).astype(o_ref.dtype)


def _gelu_ln_proj_kernel(x_ref, w_ref, b_ref, g_ref, be_ref, o_ref, last_ref):
    z = _gelu_tanh(_dot(x_ref[...], w_ref[...]) + b_ref[...])
    v = _layer_norm(z, g_ref[...], be_ref[...])
    o_ref[...] = v.astype(o_ref.dtype)
    last_ref[...] = v


def _gmlp_in(xb, w_u, b_u, w_v, b_v, vn_g, vn_b):
    m, d = xb.shape
    tm = _row_tile(m, (384, 256))
    n = w_u.shape[1]
    row = pl.BlockSpec((tm, d), lambda i: (i, 0))
    wsp = pl.BlockSpec((d, n), lambda i: (0, 0))
    vec = pl.BlockSpec((1, n), lambda i: (0, 0))
    out = pl.BlockSpec((tm, n), lambda i: (i, 0))
    u = pl.pallas_call(
        _gelu_proj_kernel, grid=(m // tm,), in_specs=[row, wsp, vec], out_specs=out,
        out_shape=jax.ShapeDtypeStruct((m, n), BF16), compiler_params=_params("arbitrary"),
        name="gmlp_u")(xb, w_u, b_u)
    v, v_last = pl.pallas_call(
        _gelu_ln_proj_kernel, grid=(m // tm,), in_specs=[row, wsp, vec, vec, vec],
        out_specs=[out, pl.BlockSpec((tm, n), lambda i: (0, 0))],
        out_shape=[jax.ShapeDtypeStruct((m, n), BF16), jax.ShapeDtypeStruct((tm, n), F32)],
        compiler_params=_params("arbitrary"), name="gmlp_v")(xb, w_v, b_v, vn_g, vn_b)
    return u, v, v_last


def _qkv_proj_kernel(x_ref, w_ref, *refs, n_scaled, scale, n_skip32):
    o_ref = refs[0]
    j = pl.program_id(1)
    acc = _dot(x_ref[...], w_ref[...])

    @pl.when(j < n_scaled)
    def _():
        o_ref[...] = (acc * scale).astype(o_ref.dtype)

    @pl.when(j >= n_scaled)
    def _():
        o_ref[...] = acc.astype(o_ref.dtype)

    if len(refs) > 1:
        @pl.when(j >= n_skip32)
        def _():
            refs[1][...] = acc


def _qkv_proj(xb, w, *, n_scaled, scale, f32_from=None, tn=1024):
    m, d = xb.shape
    n = w.shape[1]
    tm = _row_tile(m, (768, 512, 256))
    in_specs = [pl.BlockSpec((tm, d), lambda i, j: (i, 0)), pl.BlockSpec((d, tn), lambda i, j: (0, j))]
    out_specs = [pl.BlockSpec((tm, tn), lambda i, j: (i, j))]
    out_shape = [jax.ShapeDtypeStruct((m, n), BF16)]
    if f32_from is not None:
        out_specs.append(pl.BlockSpec((tm, tn), lambda i, j: (i, jnp.maximum(j - f32_from, 0))))
        out_shape.append(jax.ShapeDtypeStruct((m, n - f32_from * tn), F32))
    return pl.pallas_call(
        functools.partial(_qkv_proj_kernel, n_scaled=n_scaled, scale=scale,
                          n_skip32=f32_from if f32_from is not None else 0),
        grid=(m // tm, n // tn), in_specs=in_specs, out_specs=out_specs, out_shape=out_shape,
        compiler_params=_params("arbitrary", "arbitrary"), name="qkv_proj")(xb, w)


def _rot_proj_kernel(x_ref, w_ref, cos_ref, sin_ref, o_ref, *, n_q, n_k, k_scale, head):
    j = pl.program_id(1)
    acc = _dot(x_ref[...], w_ref[...])
    half = head // 2

    def rotate(scale):
        cos, sin = cos_ref[...], sin_ref[...]
        for h in range(acc.shape[1] // head):
            x1 = acc[:, h * head:h * head + half]
            x2 = acc[:, h * head + half:(h + 1) * head]
            o_ref[:, h * head:h * head + half] = ((x1 * cos - x2 * sin) * scale).astype(o_ref.dtype)
            o_ref[:, h * head + half:(h + 1) * head] = ((x1 * sin + x2 * cos) * scale).astype(o_ref.dtype)

    @pl.when(j < n_q)
    def _():
        rotate(1.0)

    @pl.when(jnp.logical_and(j >= n_q, j < n_q + n_k))
    def _():
        rotate(k_scale)

    @pl.when(j >= n_q + n_k)
    def _():
        o_ref[...] = acc.astype(o_ref.dtype)


def _rot_proj(xb, w, cos, sin, *, n_q, n_k, k_scale, head, tn=1024):
    m, d = xb.shape
    n = w.shape[1]
    tm = _row_tile(m, (768, 512, 256))
    half = head // 2
    return pl.pallas_call(
        functools.partial(_rot_proj_kernel, n_q=n_q, n_k=n_k, k_scale=k_scale, head=head),
        grid=(m // tm, n // tn),
        in_specs=[pl.BlockSpec((tm, d), lambda i, j: (i, 0)), pl.BlockSpec((d, tn), lambda i, j: (0, j)),
                  pl.BlockSpec((tm, half), lambda i, j: (i, 0)), pl.BlockSpec((tm, half), lambda i, j: (i, 0))],
        out_specs=pl.BlockSpec((tm, tn), lambda i, j: (i, j)),
        out_shape=jax.ShapeDtypeStruct((m, n), BF16),
        compiler_params=_params("arbitrary", "arbitrary"), name="rot_proj")(xb, w, cos, sin)


def _mlstm_gates_kernel(x_ref, w_ref, b_ref, o_ref, *, heads):
    g = _dot(x_ref[...], w_ref[...]) + b_ref[...]
    lane = lax.broadcasted_iota(jnp.int32, g.shape, 1)
    o_ref[...] = jnp.where(lane < heads, g, _log_sigmoid(g))


def _mlstm_gates(xb, w, b, heads):
    m, d = xb.shape
    tm = ROW_TILE
    return pl.pallas_call(
        functools.partial(_mlstm_gates_kernel, heads=heads), grid=(m // tm,),
        in_specs=[pl.BlockSpec((tm, d), lambda i: (i, 0)), pl.BlockSpec((d, 128), lambda i: (0, 0)),
                  pl.BlockSpec((1, 128), lambda i: (0, 0))],
        out_specs=pl.BlockSpec((tm, 128), lambda i: (i, 0)),
        out_shape=jax.ShapeDtypeStruct((m, 128), F32),
        compiler_params=_params("arbitrary"), name="mlstm_gates")(xb, w, b)


def _fox_gates_kernel(x_ref, w_ref, b_ref, tri_ref, lf_ref, c_ref, carry_ref, *, tiles_per_seq):
    i = pl.program_id(0)
    lf = _log_sigmoid(_dot(x_ref[...], w_ref[...]) + b_ref[...])
    lf_ref[...] = lf

    @pl.when(i % tiles_per_seq == 0)
    def _():
        carry_ref[...] = jnp.zeros_like(carry_ref)

    c = _dot_exact_lhs(tri_ref[...], lf) + carry_ref[...]
    c_ref[...] = c
    carry_ref[...] = c[-1:, :]


def _fox_gates(xb, w, b, tri, tiles_per_seq, n_prompt_tiles):
    m, d = xb.shape
    tm = ROW_TILE
    out = pl.BlockSpec((tm, 128), lambda i: (i, 0))
    return pl.pallas_call(
        functools.partial(_fox_gates_kernel, tiles_per_seq=tiles_per_seq), grid=(m // tm,),
        in_specs=[pl.BlockSpec((tm, d), lambda i: (i, 0)), pl.BlockSpec((d, 128), lambda i: (0, 0)),
                  pl.BlockSpec((1, 128), lambda i: (0, 0)),
                  pl.BlockSpec((None, tm, tm), lambda i: (i // n_prompt_tiles, 0, 0))],
        out_specs=[out, out],
        out_shape=[jax.ShapeDtypeStruct((m, 128), F32)] * 2,
        scratch_shapes=[pltpu.VMEM((1, 128), F32)],
        compiler_params=_params("arbitrary"), name="fox_gates")(xb, w, b, tri)


def _out_ln_kernel(h_ref, w_ref, x_ref, g_ref, b_ref, o32_ref, o16_ref, *, alpha):
    y = alpha * x_ref[...] + _dot(h_ref[...], w_ref[...])
    y = _layer_norm(y, g_ref[...], b_ref[...])
    o32_ref[...] = y
    o16_ref[...] = y.astype(BF16)


def _out_ln(h, w, x32, g, b, alpha):
    m, k = h.shape
    d = w.shape[1]
    tm = _row_tile(m, (384, 256))
    vec = pl.BlockSpec((1, d), lambda i: (0, 0))
    row = pl.BlockSpec((tm, d), lambda i: (i, 0))
    return pl.pallas_call(
        functools.partial(_out_ln_kernel, alpha=alpha), grid=(m // tm,),
        in_specs=[pl.BlockSpec((tm, k), lambda i: (i, 0)),
                  pl.BlockSpec((k, d), lambda i: (0, 0), pipeline_mode=pl.Buffered(1)),
                  row, vec, vec],
        out_specs=[row, row],
        out_shape=[jax.ShapeDtypeStruct((m, d), F32), jax.ShapeDtypeStruct((m, d), BF16)],
        compiler_params=_params("arbitrary"), name="out_ln")(h, w, x32, g, b)


def _ffn_kernel(xb_ref, wg_ref, wu_ref, wo_ref, x_ref, g_ref, b_ref, o32_ref, o16_ref, acc_ref, *, alpha):
    j = pl.program_id(1)

    @pl.when(j == 0)
    def _():
        acc_ref[...] = jnp.zeros_like(acc_ref)

    xb = xb_ref[...]
    gate = _dot(xb, wg_ref[...])
    up = _dot(xb, wu_ref[...])
    hid = (gate * _sigmoid(gate) * up).astype(BF16)
    acc_ref[...] += _dot(hid, wo_ref[...])

    @pl.when(j == pl.num_programs(1) - 1)
    def _():
        y = _layer_norm(alpha * x_ref[...] + acc_ref[...], g_ref[...], b_ref[...])
        o32_ref[...] = y
        o16_ref[...] = y.astype(BF16)


def _ffn(xb, x32, w_in, w_out, g, b, alpha, th=512):
    m, d = xb.shape
    hidden = w_out.shape[0]
    tm = _row_tile(m, (384, 256))
    nh = hidden // th
    vec = pl.BlockSpec((1, d), lambda i, j: (0, 0))
    row = pl.BlockSpec((tm, d), lambda i, j: (i, 0))
    return pl.pallas_call(
        functools.partial(_ffn_kernel, alpha=alpha), grid=(m // tm, nh),
        in_specs=[row, pl.BlockSpec((d, th), lambda i, j: (0, j)),
                  pl.BlockSpec((d, th), lambda i, j: (0, j + nh)),
                  pl.BlockSpec((th, d), lambda i, j: (j, 0)), row, vec, vec],
        out_specs=[row, row],
        out_shape=[jax.ShapeDtypeStruct((m, d), F32), jax.ShapeDtypeStruct((m, d), BF16)],
        scratch_shapes=[pltpu.VMEM((tm, d), F32)],
        compiler_params=_params("arbitrary", "arbitrary"), name="ffn")(xb, w_in, w_in, w_out, x32, g, b)


def _gmlp_gate_kernel(u_ref, v_ref, w_ref, b_ref, y_ref, *, groups):
    gd = u_ref.shape[1] // groups
    for g in range(groups):
        sl = slice(g * gd, (g + 1) * gd)
        s = _dot(w_ref[g], v_ref[:, sl]) + b_ref[:, sl]
        y_ref[:, sl] = (u_ref[:, sl].astype(F32) * s).astype(y_ref.dtype)


def _gmlp_gate(u, v, w_sp, b_sp, n_prompt_tiles):
    m, d = u.shape
    tm = ROW_TILE
    groups = w_sp.shape[1]
    row = pl.BlockSpec((tm, d), lambda i: (i, 0))
    return pl.pallas_call(
        functools.partial(_gmlp_gate_kernel, groups=groups), grid=(m // tm,),
        in_specs=[row, row,
                  pl.BlockSpec((None, groups, tm, tm), lambda i: (i // n_prompt_tiles, 0, 0, 0)),
                  pl.BlockSpec((None, tm, d), lambda i: (i // n_prompt_tiles, 0, 0))],
        out_specs=row, out_shape=jax.ShapeDtypeStruct((m, d), BF16),
        compiler_params=_params("arbitrary"), name="gmlp_gate")(u, v, w_sp, b_sp)


def _mlstm_kernel(q_ref, k_ref, v_ref, og_ref, gt_ref, ng_ref, c0_ref, n0_ref, m0_ref, *refs, heads):
    h_ref, c_ref, n_ref, m_ref = refs[-4:]
    step = pl.program_id(1)
    L = q_ref.shape[0]
    dqk = q_ref.shape[1] // heads
    dv = v_ref.shape[1] // heads

    @pl.when(step == 0)
    def _():
        c_ref[...] = c0_ref[...]
        n_ref[...] = n0_ref[...]
        m_ref[...] = m0_ref[...]

    g = gt_ref[...]
    row = lax.broadcasted_iota(jnp.int32, (L, L), 0)
    col = lax.broadcasted_iota(jnp.int32, (L, L), 1)
    causal = col <= row
    bc = _dot_exact_lhs(causal.astype(BF16), g)
    g_t = _transpose_exact(g)
    bc_t = _transpose_exact(bc)

    for h in range(heads):
        a = bc[:, heads + h:heads + h + 1]
        li = g[:, h:h + 1]
        a_row = bc_t[heads + h:heads + h + 1, :]
        li_row = g_t[h:h + 1, :]
        m_prev = m_ref[h:h + 1, :]
        dmat = jnp.where(causal, a - a_row + li_row, -jnp.inf)
        inter = a + m_prev
        m_t = jnp.maximum(inter, jnp.max(dmat, axis=1, keepdims=True))
        w_intra = jnp.exp(dmat - m_t)
        w_inter = jnp.exp(inter - m_t)
        qh = q_ref[:, h * dqk:(h + 1) * dqk]
        kh = k_ref[:, h * dqk:(h + 1) * dqk]
        vh = v_ref[:, h * dv:(h + 1) * dv]
        c_old = c_ref[h]
        n_old = n_ref[h:h + 1, :]
        scores = _dg(qh, kh, NT) * w_intra
        num = _dot(scores.astype(BF16), vh) + w_inter * _dg(qh, c_old.astype(BF16), NT)
        den = (jnp.sum(scores, axis=1, keepdims=True)
               + w_inter * jnp.sum(qh.astype(F32) * n_old, axis=1, keepdims=True))
        hid = num / jnp.maximum(jnp.abs(den), jnp.exp(-m_t))
        sl = slice(h * dv, (h + 1) * dv)
        y = hid * lax.rsqrt(jnp.mean(hid * hid, axis=1, keepdims=True) + LN_EPS) * ng_ref[:, sl]
        h_ref[:, sl] = (y * _sigmoid(og_ref[:, sl].astype(F32))).astype(h_ref.dtype)

        m_new = m_t[L - 1:L, :]
        b_last = a[L - 1:L, :]
        w_c = jnp.exp(b_last + m_prev - m_new)
        w_s = jnp.exp(b_last - a + li - m_new)
        c_ref[h] = w_c * c_old + _dg((vh.astype(F32) * w_s).astype(BF16), kh, TN)
        n_ref[h:h + 1, :] = w_c * n_old + jnp.sum(kh.astype(F32) * w_s, axis=0, keepdims=True)
        m_ref[h:h + 1, :] = m_new


def _mlstm(qkv, gates, norm_g, c0, n0, m0, *, chunk, row0, seq_len, prev_out=None):
    m, width = qkv.shape
    d = width // 3
    nb, heads, dv, dqk = c0.shape
    nc = seq_len // chunk
    base = row0 // chunk

    def rb(b, c):
        return base + b * nc + c

    in_specs = [pl.BlockSpec((chunk, d // 2), lambda b, c: (rb(b, c), 0)),
                pl.BlockSpec((chunk, d // 2), lambda b, c: (rb(b, c), 1)),
                pl.BlockSpec((chunk, d), lambda b, c: (rb(b, c), 1)),
                pl.BlockSpec((chunk, d), lambda b, c: (rb(b, c), 2)),
                pl.BlockSpec((chunk, 128), lambda b, c: (rb(b, c), 0)),
                pl.BlockSpec((1, d), lambda b, c: (0, 0)),
                pl.BlockSpec((None, heads, dv, dqk), lambda b, c: (b, 0, 0, 0)),
                pl.BlockSpec((None, heads, dqk), lambda b, c: (b, 0, 0)),
                pl.BlockSpec((None, heads, 1), lambda b, c: (b, 0, 0))]
    args = [qkv, qkv, qkv, qkv, gates, norm_g, c0, n0, m0.reshape(nb, heads, 1)]
    aliases = {}
    if prev_out is not None:
        in_specs.append(pl.BlockSpec(memory_space=pl.ANY))
        args.append(prev_out)
        aliases = {len(args) - 1: 0}
    outs = pl.pallas_call(
        functools.partial(_mlstm_kernel, heads=heads), grid=(nb, nc), in_specs=in_specs,
        out_specs=[pl.BlockSpec((chunk, d), lambda b, c: (rb(b, c), 0)),
                   pl.BlockSpec((None, heads, dv, dqk), lambda b, c: (b, 0, 0, 0)),
                   pl.BlockSpec((None, heads, dqk), lambda b, c: (b, 0, 0)),
                   pl.BlockSpec((None, heads, 1), lambda b, c: (b, 0, 0))],
        out_shape=[jax.ShapeDtypeStruct((m, d), BF16), jax.ShapeDtypeStruct(c0.shape, F32),
                   jax.ShapeDtypeStruct(n0.shape, F32), jax.ShapeDtypeStruct((nb, heads, 1), F32)],
        input_output_aliases=aliases,
        compiler_params=_params("arbitrary", "arbitrary"), name="mlstm")(*args)
    return outs[0], outs[1], outs[2], outs[3].reshape(nb, heads)


def _retention_kernel(q_ref, k_ref, v_ref, g_ref, gg_ref, gb_ref, s0_ref, *refs, heads):
    o_ref, s_ref = refs[-2:]
    step = pl.program_id(1)
    L = q_ref.shape[0]
    dk = q_ref.shape[1] // heads
    dv = v_ref.shape[1] // heads

    @pl.when(step == 0)
    def _():
        s_ref[...] = s0_ref[...]

    row = lax.broadcasted_iota(jnp.int32, (L, L), 0)
    col = lax.broadcasted_iota(jnp.int32, (L, L), 1)
    causal = col <= row
    lag = (row - col).astype(F32)
    t = lax.broadcasted_iota(jnp.int32, (L, 1), 0).astype(F32)

    for h in range(heads):
        log_gamma = math.log1p(-(2.0 ** (-5.0 - h)))
        decay = jnp.where(causal, jnp.exp(lag * log_gamma), 0.0)
        decay_q = jnp.exp((t + 1.0) * log_gamma)
        decay_k = jnp.exp((L - 1.0 - t) * log_gamma)
        decay_s = math.exp(L * log_gamma)
        qh = q_ref[:, h * dk:(h + 1) * dk]
        kh = k_ref[:, h * dk:(h + 1) * dk]
        vh = v_ref[:, h * dv:(h + 1) * dv]
        s_old = s_ref[h]
        scores = _dg(qh, kh, NT) * decay
        o = _dot(scores.astype(BF16), vh) + _dot(qh, s_old.astype(BF16)) * decay_q
        s_ref[h] = decay_s * s_old + _dg((kh.astype(F32) * decay_k).astype(BF16), vh, TN)
        sl = slice(h * dv, (h + 1) * dv)
        y = _layer_norm(o, gg_ref[:, sl], gb_ref[:, sl])
        gate = g_ref[:, sl].astype(F32)
        o_ref[:, sl] = (gate * _sigmoid(gate) * y).astype(o_ref.dtype)


def _retention(qkvg, gn_g, gn_b, s0, *, chunk, row0, seq_len, prev_out=None):
    m, width = qkvg.shape
    d = width // 6
    nb, heads, dk, dv = s0.shape
    nc = seq_len // chunk
    base = row0 // chunk

    def rb(b, c):
        return base + b * nc + c

    in_specs = [pl.BlockSpec((chunk, d), lambda b, c: (rb(b, c), 0)),
                pl.BlockSpec((chunk, d), lambda b, c: (rb(b, c), 1)),
                pl.BlockSpec((chunk, 2 * d), lambda b, c: (rb(b, c), 1)),
                pl.BlockSpec((chunk, 2 * d), lambda b, c: (rb(b, c), 2)),
                pl.BlockSpec((1, 2 * d), lambda b, c: (0, 0)),
                pl.BlockSpec((1, 2 * d), lambda b, c: (0, 0)),
                pl.BlockSpec((None, heads, dk, dv), lambda b, c: (b, 0, 0, 0))]
    args = [qkvg, qkvg, qkvg, qkvg, gn_g, gn_b, s0]
    aliases = {}
    if prev_out is not None:
        in_specs.append(pl.BlockSpec(memory_space=pl.ANY))
        args.append(prev_out)
        aliases = {len(args) - 1: 0}
    return pl.pallas_call(
        functools.partial(_retention_kernel, heads=heads), grid=(nb, nc), in_specs=in_specs,
        out_specs=[pl.BlockSpec((chunk, 2 * d), lambda b, c: (rb(b, c), 0)),
                   pl.BlockSpec((None, heads, dk, dv), lambda b, c: (b, 0, 0, 0))],
        out_shape=[jax.ShapeDtypeStruct((m, 2 * d), BF16), jax.ShapeDtypeStruct(s0.shape, F32)],
        input_output_aliases=aliases,
        compiler_params=_params("arbitrary", "arbitrary"), name="retention")(*args)


def _select_lane(block, lane_index):
    lane = lax.broadcasted_iota(jnp.int32, block.shape, 1)
    return jnp.sum(jnp.where(lane == lane_index, block, 0.0), axis=1, keepdims=True)


def _fox_prompt_kernel(q_ref, k_ref, v_ref, c_ref, crow_ref, o_ref, m_scr, l_scr, acc_scr, *, tk):
    h = pl.program_id(1)
    qi = pl.program_id(2)
    tq = q_ref.shape[0]
    q = q_ref[...]
    cq = _select_lane(c_ref[...], h)
    m_scr[...] = jnp.full_like(m_scr, -jnp.inf)
    l_scr[...] = jnp.zeros_like(l_scr)
    acc_scr[...] = jnp.zeros_like(acc_scr)

    def update(ks, diag_offset):
        kb = k_ref[pl.ds(ks, tk), :]
        vb = v_ref[pl.ds(ks, tk), :]
        s = _dg(q, kb, NT) + cq - crow_ref[:, pl.ds(ks, tk)]
        if diag_offset is not None:
            row = lax.broadcasted_iota(jnp.int32, s.shape, 0)
            col = lax.broadcasted_iota(jnp.int32, s.shape, 1)
            s = jnp.where(col + diag_offset <= row, s, -jnp.inf)
        m_old = m_scr[...]
        m_new = jnp.maximum(m_old, jnp.max(s, axis=1, keepdims=True))
        p = jnp.exp(s - m_new)
        alpha = jnp.exp(m_old - m_new)
        l_scr[...] = alpha * l_scr[...] + jnp.sum(p, axis=1, keepdims=True)
        acc_scr[...] = alpha * acc_scr[...] + _dot(p.astype(BF16), vb)
        m_scr[...] = m_new

    def below_diagonal(kk, carry):
        update(pl.multiple_of(kk * tk, tk), None)
        return carry

    lax.fori_loop(0, qi * (tq // tk), below_diagonal, 0)
    for d in range(tq // tk):
        update(pl.multiple_of(qi * tq + d * tk, tk), d * tk)
    o_ref[...] = (acc_scr[...] / l_scr[...]).astype(o_ref.dtype)


def _fox_prompt(qkv, c, crow, *, heads, n_seq, seq_len, tq, tk):
    m, width = qkv.shape
    d = width // 3
    hd = d // heads
    nq = seq_len // tq
    return pl.pallas_call(
        functools.partial(_fox_prompt_kernel, tk=tk), grid=(n_seq, heads, nq),
        in_specs=[pl.BlockSpec((tq, hd), lambda b, h, i: (b * nq + i, h)),
                  pl.BlockSpec((seq_len, hd), lambda b, h, i: (b, heads + h)),
                  pl.BlockSpec((seq_len, hd), lambda b, h, i: (b, 2 * heads + h)),
                  pl.BlockSpec((tq, 128), lambda b, h, i: (b * nq + i, 0)),
                  pl.BlockSpec((None, None, 1, seq_len), lambda b, h, i: (b, h, 0, 0))],
        out_specs=pl.BlockSpec((tq, hd), lambda b, h, i: (b * nq + i, h)),
        out_shape=jax.ShapeDtypeStruct((m, d), BF16),
        scratch_shapes=[pltpu.VMEM((tq, 1), F32), pltpu.VMEM((tq, 1), F32), pltpu.VMEM((tq, hd), F32)],
        compiler_params=_params("arbitrary", "arbitrary", "arbitrary"), name="fox_prompt")(qkv, qkv, qkv, c, crow)


def _fox_sample_kernel(q_ref, kn_ref, vn_ref, c_ref, kc_ref, vc_ref, lfc_ref, prev_ref, o_ref, d_scr, ct_scr):
    del prev_ref
    h = pl.program_id(1)
    past = kc_ref.shape[0]
    ts = q_ref.shape[0]

    @pl.when(h == 0)
    def _():
        later = (lax.broadcasted_iota(jnp.int32, (past, past), 0)
                 > lax.broadcasted_iota(jnp.int32, (past, past), 1)).astype(BF16)
        hi, mid, lo = _split3(lfc_ref[...])
        d_scr[...] = _dot(hi, later) + _dot(mid, later) + _dot(lo, later)
        ct_scr[...] = _transpose_exact(c_ref[...])

    cq = _select_lane(c_ref[...], h)
    d_row = d_scr[pl.ds(h, 1), :]
    cn_row = ct_scr[pl.ds(h, 1), :]
    q = q_ref[...]
    s_past = _dg(q, kc_ref[...].astype(BF16), NT) + cq + d_row
    s_new = _dg(q, kn_ref[...], NT) + cq - cn_row
    row = lax.broadcasted_iota(jnp.int32, (ts, ts), 0)
    col = lax.broadcasted_iota(jnp.int32, (ts, ts), 1)
    s_new = jnp.where(col <= row, s_new, -jnp.inf)
    m = jnp.maximum(jnp.max(s_past, axis=1, keepdims=True), jnp.max(s_new, axis=1, keepdims=True))
    p_past = jnp.exp(s_past - m)
    p_new = jnp.exp(s_new - m)
    den = jnp.sum(p_past, axis=1, keepdims=True) + jnp.sum(p_new, axis=1, keepdims=True)
    o = _dot(p_past.astype(BF16), vc_ref[...].astype(BF16)) + _dot(p_new.astype(BF16), vn_ref[...])
    o_ref[...] = (o / den).astype(o_ref.dtype)


def _fox_sample(qkv, c, k_cache, v_cache, lf_cache_t, prev_out, *, heads, row0, ts):
    m, width = qkv.shape
    d = width // 3
    hd = d // heads
    n_streams, past, _ = k_cache.shape
    base = row0 // ts
    return pl.pallas_call(
        _fox_sample_kernel, grid=(n_streams, heads),
        in_specs=[pl.BlockSpec((ts, hd), lambda b, h: (base + b, h)),
                  pl.BlockSpec((ts, hd), lambda b, h: (base + b, heads + h)),
                  pl.BlockSpec((ts, hd), lambda b, h: (base + b, 2 * heads + h)),
                  pl.BlockSpec((ts, 128), lambda b, h: (base + b, 0)),
                  pl.BlockSpec((None, past, hd), lambda b, h: (b, 0, h)),
                  pl.BlockSpec((None, past, hd), lambda b, h: (b, 0, h)),
                  pl.BlockSpec((None, heads, past), lambda b, h: (b, 0, 0)),
                  pl.BlockSpec(memory_space=pl.ANY)],
        out_specs=pl.BlockSpec((ts, hd), lambda b, h: (base + b, h)),
        out_shape=jax.ShapeDtypeStruct((m, d), BF16),
        scratch_shapes=[pltpu.VMEM((heads, past), F32), pltpu.VMEM((128, ts), F32)],
        input_output_aliases={7: 0},
        compiler_params=_params("arbitrary", "arbitrary"), name="fox_sample")(
            qkv, qkv, qkv, c, k_cache, v_cache, lf_cache_t, prev_out)


def _pad_cols(w, n):
    return jnp.pad(w, ((0, 0), (0, n - w.shape[1])))


def _row_vec(v, n=None):
    v = v.astype(F32).reshape(1, -1)
    return v if n is None else _pad_cols(v, n)


def _block_diag(blocks, copies):
    g, p, _ = blocks.shape
    eye = jnp.eye(copies, dtype=blocks.dtype)
    return jnp.einsum("ab,gpq->gapbq", eye, blocks).reshape(g, copies * p, copies * p)


def _gmlp_spatial(w_s, b_s, block, group_dim):
    pos = jnp.arange(block)
    mask = (pos[None, :] // GMLP_CHUNK) <= (pos[:, None] // GMLP_CHUNK)
    ws = jnp.where(mask[None], w_s[:, :block, :block], 0.0)
    copies = ROW_TILE // block
    bias = jnp.repeat(jnp.tile(jnp.transpose(b_s[:, :block]), (copies, 1)), group_dim, axis=1)
    return _block_diag(ws, copies).astype(BF16), bias.astype(F32)


def _cumsum_matrix(block):
    r = jnp.arange(ROW_TILE)
    return ((r[None, :] <= r[:, None]) & (r[None, :] // block == r[:, None] // block)).astype(BF16)


def kernel(x_prompt, x_sample, state_b_C, state_b_n, state_b_m, state_c_S, cache_d_k, cache_d_v, cache_d_logf, a_w_in, a_b_in, a_vn_g, a_vn_b, a_w_s, a_b_s, a_w_out, b_w_in, b_b_gates, b_norm_g, b_w_out, c_w_in, c_gn_g, c_gn_b, c_w_out, d_w_in, d_b_f, d_w_out, ffn_w_in, ffn_w_out, ln1_g, ln1_b, ln2_g, ln2_b):
    nbp, seq, d = x_prompt.shape
    nbs, ts, _ = x_sample.shape
    mp, ms = nbp * seq, nbs * ts
    past = cache_d_k.shape[2]
    depth = ffn_w_in.shape[0]
    alpha = (2.0 * depth) ** 0.25
    assert ms == ROW_TILE and ROW_TILE % ts == 0 and seq % ATTN_BLOCK == 0 and seq % SEQ_CHUNK == 0
    n_prompt_tiles = mp // ROW_TILE

    x32 = jnp.concatenate([x_prompt.reshape(mp, d), x_sample.reshape(ms, d)], axis=0)
    xb = x32.astype(BF16)

    a_vs = []
    b_cp, b_np, b_mp, b_cs, b_ns, b_ms = [], [], [], [], [], []
    c_sp, c_ss = [], []
    d_kp, d_vp, d_fp, d_ks, d_vs, d_fs = [], [], [], [], [], []

    for i in range(depth):
        kind, j = i % 4, i // 4
        if kind == 0:
            w_in = a_w_in[j].astype(BF16)
            u, v, v_last = _gmlp_in(xb, w_in[:, :d], _row_vec(a_b_in[j, :d]), w_in[:, d:], _row_vec(a_b_in[j, d:]),
                                    _row_vec(a_vn_g[j]), _row_vec(a_vn_b[j]))
            gd = d // GMLP_GROUPS
            wp, bp = _gmlp_spatial(a_w_s[j], a_b_s[j], GMLP_BLOCK, gd)
            wsm, bsm = _gmlp_spatial(a_w_s[j], a_b_s[j], ts, gd)
            mix = _gmlp_gate(u, v, jnp.stack([wp, wsm]), jnp.stack([bp, bsm]), n_prompt_tiles)
            w_out = a_w_out[j]
            a_vs.append(v_last[v_last.shape[0] - ms:].reshape(nbs, ts, d))
        elif kind == 1:
            heads = MLSTM_HEADS
            n_main = b_w_in.shape[2] - 2 * heads
            dqk = (n_main - 2 * d) // (2 * heads)
            qkv = _qkv_proj(xb, b_w_in[j, :, :n_main].astype(BF16), n_scaled=(heads * dqk) // 1024,
                            scale=dqk ** -0.5)[0]
            gates = _mlstm_gates(xb, _pad_cols(b_w_in[j, :, n_main:], 128).astype(BF16),
                                 _row_vec(b_b_gates[j], 128), heads)
            norm_g = _row_vec(b_norm_g[j])
            zc = jnp.zeros((nbp,) + state_b_C.shape[2:], F32)
            zn = jnp.zeros((nbp,) + state_b_n.shape[2:], F32)
            zm = jnp.zeros((nbp,) + state_b_m.shape[2:], F32)
            mix, cp, np_, mp_state = _mlstm(qkv, gates, norm_g, zc, zn, zm, chunk=SEQ_CHUNK, row0=0, seq_len=seq)
            mix, cs, ns, ms_state = _mlstm(qkv, gates, norm_g, state_b_C[j], state_b_n[j], state_b_m[j],
                                           chunk=ts, row0=mp, seq_len=ts, prev_out=mix)
            b_cp.append(cp); b_np.append(np_); b_mp.append(mp_state)
            b_cs.append(cs); b_ns.append(ns); b_ms.append(ms_state)
            w_out = b_w_out[j]
        elif kind == 2:
            heads = RET_HEADS
            dk = d // heads
            half = dk // 2
            inv = ROPE_BASE ** (-jnp.arange(half, dtype=F32) / half)
            pos = jnp.concatenate([jnp.tile(jnp.arange(seq), nbp), past + jnp.tile(jnp.arange(ts), nbs)])
            ang = pos.astype(F32)[:, None] * inv[None, :]
            qkvg = _rot_proj(xb, c_w_in[j].astype(BF16), jnp.cos(ang), jnp.sin(ang),
                             n_q=d // 1024, n_k=d // 1024, k_scale=dk ** -0.5, head=dk)
            gn_g, gn_b = _row_vec(c_gn_g[j]), _row_vec(c_gn_b[j])
            zs = jnp.zeros((nbp,) + state_c_S.shape[2:], F32)
            mix, sp = _retention(qkvg, gn_g, gn_b, zs, chunk=SEQ_CHUNK, row0=0, seq_len=seq)
            mix, ss = _retention(qkvg, gn_g, gn_b, state_c_S[j], chunk=ts, row0=mp, seq_len=ts, prev_out=mix)
            c_sp.append(sp); c_ss.append(ss)
            w_out = c_w_out[j]
        else:
            heads = FOX_HEADS
            hd = d // heads
            qkv, kv32 = _qkv_proj(xb, d_w_in[j, :, :3 * d].astype(BF16), n_scaled=d // 1024, scale=hd ** -0.5,
                                  f32_from=d // 1024)
            tri = jnp.stack([_cumsum_matrix(ROW_TILE), _cumsum_matrix(ts)])
            logf, c = _fox_gates(xb, _pad_cols(d_w_in[j, :, 3 * d:], 128).astype(BF16), _row_vec(d_b_f[j], 128),
                                 tri, seq // ROW_TILE, n_prompt_tiles)
            crow = jnp.transpose(c[:mp, :heads].reshape(nbp, seq, heads), (0, 2, 1)).reshape(nbp, heads, 1, seq)
            mix = _fox_prompt(qkv, c, crow, heads=heads, n_seq=nbp, seq_len=seq, tq=ATTN_BLOCK, tk=ATTN_BLOCK)
            mix = _fox_sample(qkv, c, cache_d_k[j].reshape(nbs, past, d), cache_d_v[j].reshape(nbs, past, d),
                              jnp.transpose(cache_d_logf[j], (0, 2, 1)), mix, heads=heads, row0=mp, ts=ts)
            k32, v32 = kv32[:, :d], kv32[:, d:]
            d_kp.append(k32[:mp].reshape(nbp, seq, heads, hd)); d_ks.append(k32[mp:].reshape(nbs, ts, heads, hd))
            d_vp.append(v32[:mp].reshape(nbp, seq, heads, hd)); d_vs.append(v32[mp:].reshape(nbs, ts, heads, hd))
            d_fp.append(logf[:mp, :heads].reshape(nbp, seq, heads))
            d_fs.append(logf[mp:, :heads].reshape(nbs, ts, heads))
            w_out = d_w_out[j]
        x32, xb = _out_ln(mix, w_out.astype(BF16), x32, _row_vec(ln1_g[i]), _row_vec(ln1_b[i]), alpha)
        x32, xb = _ffn(xb, x32, ffn_w_in[i].astype(BF16), ffn_w_out[i].astype(BF16),
                       _row_vec(ln2_g[i]), _row_vec(ln2_b[i]), alpha)

    return (x32[:mp].reshape(nbp, seq, d), x32[mp:].reshape(nbs, ts, d), jnp.stack(a_vs),
            jnp.stack(b_cp), jnp.stack(b_np), jnp.stack(b_mp),
            jnp.stack(b_cs), jnp.stack(b_ns), jnp.stack(b_ms),
            jnp.stack(c_sp), jnp.stack(c_ss),
            jnp.stack(d_kp), jnp.stack(d_vp), jnp.stack(d_fp),
            jnp.stack(d_ks), jnp.stack(d_vs), jnp.stack(d_fs))
```

```python
import functools
import math

import jax
import jax.numpy as jnp
from jax import lax
from jax.experimental import pallas as pl
from jax.experimental.pallas import tpu as pltpu

F32 = jnp.float32
BF16 = jnp.bfloat16

LN_EPS = 1e-5
ROPE_BASE = 10000.0
GMLP_BLOCK = 128
GMLP_CHUNK = 64
GMLP_GROUPS = 8
MLSTM_HEADS = 8
RET_HEADS = 8
FOX_HEADS = 16
ROW_TILE = 256
SEQ_CHUNK = 256
ATTN_BLOCK = 512
ATTN_HEADS_PER_STEP = 2
ATTN_KEY_SUB = 256
LOG2E = math.log2(math.e)
VMEM_LIMIT = 56 * 2 ** 20

NT = (((1,), (1,)), ((), ()))
TN = (((0,), (0,)), ((), ()))


def _params(*sem):
    return pltpu.CompilerParams(dimension_semantics=sem, vmem_limit_bytes=VMEM_LIMIT)


def _row_tile(m, candidates):
    for t in candidates:
        if m % t == 0:
            return t
    raise ValueError(f"no row tile for {m}")


def _dot(a, b):
    return jnp.dot(a, b, preferred_element_type=F32)


def _dg(a, b, dims):
    return lax.dot_general(a, b, dims, preferred_element_type=F32)


def _split3(x):
    hi = x.astype(BF16)
    r1 = x - hi.astype(F32)
    mid = r1.astype(BF16)
    lo = (r1 - mid.astype(F32)).astype(BF16)
    return hi, mid, lo


def _dot_exact_lhs(mat_bf16, x):
    hi, mid, lo = _split3(x)
    return _dot(mat_bf16, hi) + _dot(mat_bf16, mid) + _dot(mat_bf16, lo)


def _transpose_exact(x):
    n = x.shape[1]
    eye = (lax.broadcasted_iota(jnp.int32, (n, n), 0) == lax.broadcasted_iota(jnp.int32, (n, n), 1)).astype(BF16)
    hi, mid, lo = _split3(x)
    return _dg(eye, hi, NT) + _dg(eye, mid, NT) + _dg(eye, lo, NT)


def _sigmoid(x):
    return 1.0 / (1.0 + jnp.exp(-x))


def _log_sigmoid(x):
    return jnp.minimum(x, 0.0) - jnp.log1p(jnp.exp(-jnp.abs(x)))


def _gelu_tanh(x):
    return 0.5 * x * (1.0 + jnp.tanh(math.sqrt(2.0 / math.pi) * (x + 0.044715 * (x * x * x))))


def _layer_norm(y, g, b):
    mu = jnp.mean(y, axis=-1, keepdims=True)
    d = y - mu
    var = jnp.mean(d * d, axis=-1, keepdims=True)
    return d * lax.rsqrt(var + LN_EPS) * g + b


def _matmul_row_parts(lhs_ref, w_ref, epilogue, cast=False):
    tm = lhs_ref.shape[0]
    n_parts = 3 if tm % 48 == 0 else 2
    part = tm // n_parts
    rows = [slice(p * part, (p + 1) * part) for p in range(n_parts)]

    def matmul(r):
        lhs = lhs_ref[r, :]
        return _dot(lhs.astype(BF16) if cast else lhs, w_ref[...])

    acc = matmul(rows[0])
    for p, r in enumerate(rows):
        nxt = matmul(rows[p + 1]) if p + 1 < n_parts else None
        epilogue(r, acc)
        acc = nxt


def _cast_rows_once(x_ref, xb_scr, j):
    @pl.when(j == 0)
    def _():
        xb_scr[...] = x_ref[...].astype(BF16)


def _gelu_proj_kernel(x_ref, w_ref, b_ref, o_ref):
    def epilogue(r, acc):
        o_ref[r, :] = _gelu_tanh(acc + b_ref[...]).astype(o_ref.dtype)

    _matmul_row_parts(x_ref, w_ref, epilogue, cast=True)


def _gelu_ln_proj_kernel(x_ref, w_ref, b_ref, g_ref, be_ref, o_ref, last_ref):
    def epilogue(r, acc):
        v = _layer_norm(_gelu_tanh(acc + b_ref[...]), g_ref[...], be_ref[...])
        o_ref[r, :] = v.astype(o_ref.dtype)
        last_ref[r, :] = v

    _matmul_row_parts(x_ref, w_ref, epilogue, cast=True)


def _gmlp_in(x, w_u, b_u, w_v, b_v, vn_g, vn_b):
    m, d = x.shape
    tm = _row_tile(m, (384, 256))
    n = w_u.shape[1]
    row = pl.BlockSpec((tm, d), lambda i: (i, 0))
    wsp = pl.BlockSpec((d, n), lambda i: (0, 0))
    vec = pl.BlockSpec((1, n), lambda i: (0, 0))
    out = pl.BlockSpec((tm, n), lambda i: (i, 0))
    u = pl.pallas_call(
        _gelu_proj_kernel, grid=(m // tm,), in_specs=[row, wsp, vec], out_specs=out,
        out_shape=jax.ShapeDtypeStruct((m, n), BF16), compiler_params=_params("arbitrary"),
        name="gmlp_u")(x, w_u, b_u)
    v, v_last = pl.pallas_call(
        _gelu_ln_proj_kernel, grid=(m // tm,), in_specs=[row, wsp, vec, vec, vec],
        out_specs=[out, pl.BlockSpec((tm, n), lambda i: (0, 0))],
        out_shape=[jax.ShapeDtypeStruct((m, n), BF16), jax.ShapeDtypeStruct((tm, n), F32)],
        compiler_params=_params("arbitrary"), name="gmlp_v")(x, w_v, b_v, vn_g, vn_b)
    return u, v, v_last


def _qkv_proj_kernel(x_ref, w_ref, o_ref, *refs, n_scaled, scale, nb, tail):
    kv_refs, xb_scr = refs[:-1], refs[-1]
    j = pl.program_id(1)
    _cast_rows_once(x_ref, xb_scr, j)
    col_scale = jnp.where(j < n_scaled, scale, 1.0).astype(F32)
    accs = []

    def epilogue(r, acc):
        o_ref[r, :] = (acc * col_scale).astype(o_ref.dtype)
        accs.append(acc)

    _matmul_row_parts(xb_scr, w_ref, epilogue)

    if kv_refs:
        kp_ref, ks_ref, vp_ref, vs_ref = kv_refs
        tm, tn = o_ref.shape
        acc = jnp.concatenate(accs, axis=0)

        @pl.when(jnp.logical_and(j >= nb, j < 2 * nb))
        def _():
            kp_ref[...] = acc
            ks_ref[:, pl.ds(pl.multiple_of((j - nb) * tn, tn), tn)] = acc[tm - tail:]

        @pl.when(j >= 2 * nb)
        def _():
            vp_ref[...] = acc
            vs_ref[:, pl.ds(pl.multiple_of((j - 2 * nb) * tn, tn), tn)] = acc[tm - tail:]


def _qkv_proj(x, w, *, n_scaled, scale, kv_rows=None, tn=1024):
    m, d = x.shape
    n = w.shape[1]
    tm = _row_tile(m, (768, 512, 256))
    in_specs = [pl.BlockSpec((tm, d), lambda i, j: (i, 0)), pl.BlockSpec((d, tn), lambda i, j: (0, j))]
    out_specs = [pl.BlockSpec((tm, tn), lambda i, j: (i, j))]
    out_shape = [jax.ShapeDtypeStruct((m, n), BF16)]
    nb, tail = n // (3 * tn), 0
    if kv_rows is not None:
        mp, tail = kv_rows
        assert mp + tail == m and tail <= tm
        for first in (nb, 2 * nb):
            col = functools.partial(lambda j, first: jnp.clip(j - first, 0, nb - 1), first=first)
            out_specs += [pl.BlockSpec((tm, tn), lambda i, j, col=col: (i, col(j))),
                          pl.BlockSpec((tail, n // 3), lambda i, j: (0, 0))]
            out_shape += [jax.ShapeDtypeStruct((mp, n // 3), F32), jax.ShapeDtypeStruct((tail, n // 3), F32)]
    return pl.pallas_call(
        functools.partial(_qkv_proj_kernel, n_scaled=n_scaled, scale=scale, nb=nb, tail=tail),
        grid=(m // tm, n // tn), in_specs=in_specs, out_specs=out_specs, out_shape=out_shape,
        scratch_shapes=[pltpu.VMEM((tm, d), BF16)],
        compiler_params=_params("arbitrary", "arbitrary"), name="qkv_proj")(x, w)


def _rot_proj_kernel(x_ref, w_ref, cos_ref, sin_ref, o_ref, xb_scr, *, n_q, n_k, k_scale, head):
    j = pl.program_id(1)
    _cast_rows_once(x_ref, xb_scr, j)
    half = head // 2
    rotated = j < n_q + n_k
    col_scale = jnp.where(jnp.logical_and(j >= n_q, rotated), k_scale, 1.0).astype(F32)

    def epilogue(r, acc):
        cos = jnp.where(rotated, cos_ref[r, :], 1.0)
        sin = jnp.where(rotated, sin_ref[r, :], 0.0)
        for h in range(acc.shape[1] // head):
            x1 = acc[:, h * head:h * head + half]
            x2 = acc[:, h * head + half:(h + 1) * head]
            o_ref[r, h * head:h * head + half] = ((x1 * cos - x2 * sin) * col_scale).astype(o_ref.dtype)
            o_ref[r, h * head + half:(h + 1) * head] = ((x1 * sin + x2 * cos) * col_scale).astype(o_ref.dtype)

    _matmul_row_parts(xb_scr, w_ref, epilogue)


def _rot_proj(x, w, cos, sin, *, n_q, n_k, k_scale, head, tn=1024):
    m, d = x.shape
    n = w.shape[1]
    tm = _row_tile(m, (768, 512, 256))
    half = head // 2
    return pl.pallas_call(
        functools.partial(_rot_proj_kernel, n_q=n_q, n_k=n_k, k_scale=k_scale, head=head),
        grid=(m // tm, n // tn),
        in_specs=[pl.BlockSpec((tm, d), lambda i, j: (i, 0)), pl.BlockSpec((d, tn), lambda i, j: (0, j)),
                  pl.BlockSpec((tm, half), lambda i, j: (i, 0)), pl.BlockSpec((tm, half), lambda i, j: (i, 0))],
        out_specs=pl.BlockSpec((tm, tn), lambda i, j: (i, j)),
        out_shape=jax.ShapeDtypeStruct((m, n), BF16), scratch_shapes=[pltpu.VMEM((tm, d), BF16)],
        compiler_params=_params("arbitrary", "arbitrary"), name="rot_proj")(x, w, cos, sin)


def _mlstm_gates_kernel(x_ref, w_ref, b_ref, o_ref, *, heads):
    g = _dot(x_ref[...].astype(BF16), w_ref[...]) + b_ref[...]
    lane = lax.broadcasted_iota(jnp.int32, g.shape, 1)
    o_ref[...] = jnp.where(lane < heads, g, _log_sigmoid(g))


def _mlstm_gates(x, w, b, heads):
    m, d = x.shape
    tm = ROW_TILE
    return pl.pallas_call(
        functools.partial(_mlstm_gates_kernel, heads=heads), grid=(m // tm,),
        in_specs=[pl.BlockSpec((tm, d), lambda i: (i, 0)), pl.BlockSpec((d, 128), lambda i: (0, 0)),
                  pl.BlockSpec((1, 128), lambda i: (0, 0))],
        out_specs=pl.BlockSpec((tm, 128), lambda i: (i, 0)),
        out_shape=jax.ShapeDtypeStruct((m, 128), F32),
        compiler_params=_params("arbitrary"), name="mlstm_gates")(x, w, b)


def _fox_gates_kernel(x_ref, w_ref, b_ref, tri_ref, lf_ref, c_ref, carry_ref, *, tiles_per_seq):
    i = pl.program_id(0)
    lf = _log_sigmoid(_dot(x_ref[...].astype(BF16), w_ref[...]) + b_ref[...])
    lf_ref[...] = lf

    @pl.when(i % tiles_per_seq == 0)
    def _():
        carry_ref[...] = jnp.zeros_like(carry_ref)

    c = _dot_exact_lhs(tri_ref[...], lf) + carry_ref[...]
    c_ref[...] = c
    carry_ref[...] = c[-1:, :]


def _fox_gates(x, w, b, tri, tiles_per_seq, n_prompt_tiles):
    m, d = x.shape
    tm = ROW_TILE
    out = pl.BlockSpec((tm, 128), lambda i: (i, 0))
    return pl.pallas_call(
        functools.partial(_fox_gates_kernel, tiles_per_seq=tiles_per_seq), grid=(m // tm,),
        in_specs=[pl.BlockSpec((tm, d), lambda i: (i, 0)), pl.BlockSpec((d, 128), lambda i: (0, 0)),
                  pl.BlockSpec((1, 128), lambda i: (0, 0)),
                  pl.BlockSpec((None, tm, tm), lambda i: (i // n_prompt_tiles, 0, 0))],
        out_specs=[out, out],
        out_shape=[jax.ShapeDtypeStruct((m, 128), F32)] * 2,
        scratch_shapes=[pltpu.VMEM((1, 128), F32)],
        compiler_params=_params("arbitrary"), name="fox_gates")(x, w, b, tri)


def _out_ln_kernel(h_ref, w_ref, x_ref, g_ref, b_ref, o_ref, *, alpha):
    def epilogue(r, acc):
        o_ref[r, :] = _layer_norm(alpha * x_ref[r, :] + acc, g_ref[...], b_ref[...])

    _matmul_row_parts(h_ref, w_ref, epilogue)


def _out_ln(h, w, x, g, b, alpha):
    m, k = h.shape
    d = w.shape[1]
    tm = _row_tile(m, (384, 256))
    vec = pl.BlockSpec((1, d), lambda i: (0, 0))
    row = pl.BlockSpec((tm, d), lambda i: (i, 0))
    return pl.pallas_call(
        functools.partial(_out_ln_kernel, alpha=alpha), grid=(m // tm,),
        in_specs=[pl.BlockSpec((tm, k), lambda i: (i, 0)),
                  pl.BlockSpec((k, d), lambda i: (0, 0), pipeline_mode=pl.Buffered(1)),
                  row, vec, vec],
        out_specs=row, out_shape=jax.ShapeDtypeStruct((m, d), F32),
        compiler_params=_params("arbitrary"), name="out_ln")(h, w, x, g, b)


def _ffn_kernel(x_ref, wg_ref, wu_ref, wo_ref, g_ref, b_ref, o_ref, *rest, alpha):
    tail_refs, xb_scr = rest[:-1], rest[-1]
    j = pl.program_id(1)
    _cast_rows_once(x_ref, xb_scr, j)

    @pl.when(j == 0)
    def _():
        o_ref[...] = jnp.zeros_like(o_ref)

    xb = xb_scr[...]
    gate = _dot(xb, wg_ref[...])
    up = _dot(xb, wu_ref[...])
    hid = (gate * _sigmoid(gate) * up).astype(BF16)
    o_ref[...] += _dot(hid, wo_ref[...])

    @pl.when(j == pl.num_programs(1) - 1)
    def _():
        y = _layer_norm(alpha * x_ref[...] + o_ref[...], g_ref[...], b_ref[...])
        o_ref[...] = y
        for t_ref in tail_refs:
            t_ref[...] = y[y.shape[0] - t_ref.shape[0]:]


def _ffn(x, w_in, w_out, g, b, alpha, final_rows=None, th=512):
    m, d = x.shape
    hidden = w_out.shape[0]
    tm = _row_tile(m, (768, 512, 256))
    nh = hidden // th
    vec = pl.BlockSpec((1, d), lambda i, j: (0, 0))
    row = pl.BlockSpec((tm, d), lambda i, j: (i, 0))
    if final_rows is None:
        out_specs, out_shape = row, jax.ShapeDtypeStruct((m, d), F32)
    else:
        mp, tail = final_rows
        assert mp + tail == m and tail <= tm
        out_specs = [row, pl.BlockSpec((tail, d), lambda i, j: (0, 0))]
        out_shape = [jax.ShapeDtypeStruct((mp, d), F32), jax.ShapeDtypeStruct((tail, d), F32)]
    return pl.pallas_call(
        functools.partial(_ffn_kernel, alpha=alpha), grid=(m // tm, nh),
        in_specs=[row, pl.BlockSpec((d, th), lambda i, j: (0, j)),
                  pl.BlockSpec((d, th), lambda i, j: (0, j + nh)),
                  pl.BlockSpec((th, d), lambda i, j: (j, 0)), vec, vec],
        out_specs=out_specs, out_shape=out_shape,
        scratch_shapes=[pltpu.VMEM((tm, d), BF16)],
        compiler_params=_params("arbitrary", "arbitrary"), name="ffn")(x, w_in, w_in, w_out, g, b)


def _gmlp_gate_kernel(u_ref, v_ref, w_ref, b_ref, y_ref, *, groups):
    gd = u_ref.shape[1] // groups
    for g in range(groups):
        sl = slice(g * gd, (g + 1) * gd)
        s = _dot(w_ref[g], v_ref[:, sl]) + b_ref[:, sl]
        y_ref[:, sl] = (u_ref[:, sl].astype(F32) * s).astype(y_ref.dtype)


def _gmlp_gate(u, v, w_sp, b_sp, n_prompt_tiles):
    m, d = u.shape
    tm = ROW_TILE
    groups = w_sp.shape[1]
    row = pl.BlockSpec((tm, d), lambda i: (i, 0))
    return pl.pallas_call(
        functools.partial(_gmlp_gate_kernel, groups=groups), grid=(m // tm,),
        in_specs=[row, row,
                  pl.BlockSpec((None, groups, tm, tm), lambda i: (i // n_prompt_tiles, 0, 0, 0)),
                  pl.BlockSpec((None, tm, d), lambda i: (i // n_prompt_tiles, 0, 0))],
        out_specs=row, out_shape=jax.ShapeDtypeStruct((m, d), BF16),
        compiler_params=_params("arbitrary"), name="gmlp_gate")(u, v, w_sp, b_sp)


def _mixer_out(prev_out, rows, cols):
    return jnp.zeros((rows, cols), BF16) if prev_out is None else prev_out


def _mlstm_kernel(q_ref, k_ref, v_ref, og_ref, gt_ref, ng_ref, c0_ref, n0_ref, m0_ref, *refs, heads):
    h_ref, c_ref, n_ref, m_ref = refs[-4:]
    step = pl.program_id(1)
    L = q_ref.shape[0]
    dqk = q_ref.shape[1] // heads
    dv = v_ref.shape[1] // heads

    @pl.when(step == 0)
    def _():
        c_ref[...] = c0_ref[...]
        n_ref[...] = n0_ref[...]
        m_ref[...] = m0_ref[...]

    g = gt_ref[...]
    row = lax.broadcasted_iota(jnp.int32, (L, L), 0)
    col = lax.broadcasted_iota(jnp.int32, (L, L), 1)
    causal = col <= row
    bc = _dot_exact_lhs(causal.astype(BF16), g)
    g_t = _transpose_exact(g)
    bc_t = _transpose_exact(bc)

    for h in range(heads):
        a = bc[:, heads + h:heads + h + 1]
        li = g[:, h:h + 1]
        a_row = bc_t[heads + h:heads + h + 1, :]
        li_row = g_t[h:h + 1, :]
        m_prev = m_ref[h:h + 1, :]
        dmat = jnp.where(causal, a - a_row + li_row, -jnp.inf)
        inter = a + m_prev
        m_t = jnp.maximum(inter, jnp.max(dmat, axis=1, keepdims=True))
        w_intra = jnp.exp(dmat - m_t)
        w_inter = jnp.exp(inter - m_t)
        qh = q_ref[:, h * dqk:(h + 1) * dqk]
        kh = k_ref[:, h * dqk:(h + 1) * dqk]
        vh = v_ref[:, h * dv:(h + 1) * dv]
        c_old = c_ref[h]
        n_old = n_ref[h:h + 1, :]
        scores = _dg(qh, kh, NT) * w_intra
        num = _dot(scores.astype(BF16), vh) + w_inter * _dg(qh, c_old.astype(BF16), NT)
        den = (jnp.sum(scores, axis=1, keepdims=True)
               + w_inter * jnp.sum(qh.astype(F32) * n_old, axis=1, keepdims=True))
        hid = num / jnp.maximum(jnp.abs(den), jnp.exp(-m_t))
        sl = slice(h * dv, (h + 1) * dv)
        y = hid * lax.rsqrt(jnp.mean(hid * hid, axis=1, keepdims=True) + LN_EPS) * ng_ref[:, sl]
        h_ref[:, sl] = (y * _sigmoid(og_ref[:, sl].astype(F32))).astype(h_ref.dtype)

        m_new = m_t[L - 1:L, :]
        b_last = a[L - 1:L, :]
        w_c = jnp.exp(b_last + m_prev - m_new)
        w_s = jnp.exp(b_last - a + li - m_new)
        c_ref[h] = w_c * c_old + _dg((vh.astype(F32) * w_s).astype(BF16), kh, TN)
        n_ref[h:h + 1, :] = w_c * n_old + jnp.sum(kh.astype(F32) * w_s, axis=0, keepdims=True)
        m_ref[h:h + 1, :] = m_new


def _mlstm(qkv, gates, norm_g, c0, n0, m0, *, chunk, row0, seq_len, prev_out=None):
    m, width = qkv.shape
    d = width // 3
    nb, heads, dv, dqk = c0.shape
    nc = seq_len // chunk
    base = row0 // chunk

    def rb(b, c):
        return base + b * nc + c

    in_specs = [pl.BlockSpec((chunk, d // 2), lambda b, c: (rb(b, c), 0)),
                pl.BlockSpec((chunk, d // 2), lambda b, c: (rb(b, c), 1)),
                pl.BlockSpec((chunk, d), lambda b, c: (rb(b, c), 1)),
                pl.BlockSpec((chunk, d), lambda b, c: (rb(b, c), 2)),
                pl.BlockSpec((chunk, 128), lambda b, c: (rb(b, c), 0)),
                pl.BlockSpec((1, d), lambda b, c: (0, 0)),
                pl.BlockSpec((None, heads, dv, dqk), lambda b, c: (b, 0, 0, 0)),
                pl.BlockSpec((None, heads, dqk), lambda b, c: (b, 0, 0)),
                pl.BlockSpec((None, heads, 1), lambda b, c: (b, 0, 0))]
    args = [qkv, qkv, qkv, qkv, gates, norm_g, c0, n0, m0.reshape(nb, heads, 1)]
    in_specs.append(pl.BlockSpec(memory_space=pl.ANY))
    args.append(_mixer_out(prev_out, m, d))
    aliases = {len(args) - 1: 0}
    outs = pl.pallas_call(
        functools.partial(_mlstm_kernel, heads=heads), grid=(nb, nc), in_specs=in_specs,
        out_specs=[pl.BlockSpec((chunk, d), lambda b, c: (rb(b, c), 0)),
                   pl.BlockSpec((None, heads, dv, dqk), lambda b, c: (b, 0, 0, 0)),
                   pl.BlockSpec((None, heads, dqk), lambda b, c: (b, 0, 0)),
                   pl.BlockSpec((None, heads, 1), lambda b, c: (b, 0, 0))],
        out_shape=[jax.ShapeDtypeStruct((m, d), BF16), jax.ShapeDtypeStruct(c0.shape, F32),
                   jax.ShapeDtypeStruct(n0.shape, F32), jax.ShapeDtypeStruct((nb, heads, 1), F32)],
        input_output_aliases=aliases,
        compiler_params=_params("arbitrary", "arbitrary"), name="mlstm")(*args)
    return outs[0], outs[1], outs[2], outs[3].reshape(nb, heads)


def _retention_kernel(q_ref, k_ref, v_ref, g_ref, gg_ref, gb_ref, s0_ref, *refs, heads):
    o_ref, s_ref = refs[-2:]
    step = pl.program_id(1)
    L = q_ref.shape[0]
    dk = q_ref.shape[1] // heads
    dv = v_ref.shape[1] // heads

    @pl.when(step == 0)
    def _():
        s_ref[...] = s0_ref[...]

    row = lax.broadcasted_iota(jnp.int32, (L, L), 0)
    col = lax.broadcasted_iota(jnp.int32, (L, L), 1)
    causal = col <= row
    lag = (row - col).astype(F32)
    t = lax.broadcasted_iota(jnp.int32, (L, 1), 0).astype(F32)

    for h in range(heads):
        log_gamma = math.log1p(-(2.0 ** (-5.0 - h)))
        decay = jnp.where(causal, jnp.exp(lag * log_gamma), 0.0)
        decay_q = jnp.exp((t + 1.0) * log_gamma)
        decay_k = jnp.exp((L - 1.0 - t) * log_gamma)
        decay_s = math.exp(L * log_gamma)
        qh = q_ref[:, h * dk:(h + 1) * dk]
        kh = k_ref[:, h * dk:(h + 1) * dk]
        vh = v_ref[:, h * dv:(h + 1) * dv]
        s_old = s_ref[h]
        scores = _dg(qh, kh, NT) * decay
        o = _dot(scores.astype(BF16), vh) + _dot(qh, s_old.astype(BF16)) * decay_q
        s_ref[h] = decay_s * s_old + _dg((kh.astype(F32) * decay_k).astype(BF16), vh, TN)
        sl = slice(h * dv, (h + 1) * dv)
        y = _layer_norm(o, gg_ref[:, sl], gb_ref[:, sl])
        gate = g_ref[:, sl].astype(F32)
        o_ref[:, sl] = (gate * _sigmoid(gate) * y).astype(o_ref.dtype)


def _retention(qkvg, gn_g, gn_b, s0, *, chunk, row0, seq_len, prev_out=None):
    m, width = qkvg.shape
    d = width // 6
    nb, heads, dk, dv = s0.shape
    nc = seq_len // chunk
    base = row0 // chunk

    def rb(b, c):
        return base + b * nc + c

    in_specs = [pl.BlockSpec((chunk, d), lambda b, c: (rb(b, c), 0)),
                pl.BlockSpec((chunk, d), lambda b, c: (rb(b, c), 1)),
                pl.BlockSpec((chunk, 2 * d), lambda b, c: (rb(b, c), 1)),
                pl.BlockSpec((chunk, 2 * d), lambda b, c: (rb(b, c), 2)),
                pl.BlockSpec((1, 2 * d), lambda b, c: (0, 0)),
                pl.BlockSpec((1, 2 * d), lambda b, c: (0, 0)),
                pl.BlockSpec((None, heads, dk, dv), lambda b, c: (b, 0, 0, 0))]
    args = [qkvg, qkvg, qkvg, qkvg, gn_g, gn_b, s0]
    in_specs.append(pl.BlockSpec(memory_space=pl.ANY))
    args.append(_mixer_out(prev_out, m, 2 * d))
    aliases = {len(args) - 1: 0}
    return pl.pallas_call(
        functools.partial(_retention_kernel, heads=heads), grid=(nb, nc), in_specs=in_specs,
        out_specs=[pl.BlockSpec((chunk, 2 * d), lambda b, c: (rb(b, c), 0)),
                   pl.BlockSpec((None, heads, dk, dv), lambda b, c: (b, 0, 0, 0))],
        out_shape=[jax.ShapeDtypeStruct((m, 2 * d), BF16), jax.ShapeDtypeStruct(s0.shape, F32)],
        input_output_aliases=aliases,
        compiler_params=_params("arbitrary", "arbitrary"), name="retention")(*args)


def _select_lane(block, lane_index):
    lane = lax.broadcasted_iota(jnp.int32, block.shape, 1)
    return jnp.sum(jnp.where(lane == lane_index, block, 0.0), axis=1, keepdims=True)


def _bias_tail(c_col, *, query):
    hi, mid, lo = (t.astype(F32) for t in _split3(c_col))
    lane = lax.broadcasted_iota(jnp.int32, (c_col.shape[0], 128), 1)
    if query:
        tail = jnp.where(lane == 0, hi, jnp.where(lane == 1, mid, jnp.where(lane == 2, lo,
                                                                             jnp.where(lane < 6, 1.0, 0.0))))
    else:
        tail = jnp.where(lane < 3, 1.0, jnp.where(lane == 3, -hi, jnp.where(lane == 4, -mid,
                                                                             jnp.where(lane == 5, -lo, 0.0))))
    return tail.astype(BF16)


def _fox_prompt_kernel(q_ref, k_ref, v_ref, c_ref, buf_ref, o_ref, ka_scr, vt_scr, m_scr, l_scr, acc_scr, *,
                       hps, sub):
    del buf_ref
    hg = pl.program_id(1)
    qi = pl.program_id(2)
    tq = tk = q_ref.shape[0]
    seq = k_ref.shape[0]
    hd = q_ref.shape[1] // hps

    @pl.when(qi == 0)
    def _():
        eye =(lax.broadcasted_iota(jnp.int32, (hd, hd), 0)
               == lax.broadcasted_iota(jnp.int32, (hd, hd), 1)).astype(BF16)

        def build(r, carry):
            r0 = pl.multiple_of(r * tk, tk)
            cblk = c_ref[pl.ds(r0, tk), :] * LOG2E
            for j in range(hps):
                ka_scr[j, pl.ds(r0, tk), 0:hd] = k_ref[pl.ds(r0, tk), j * hd:(j + 1) * hd]
                ka_scr[j, pl.ds(r0, tk), hd:2 * hd] = _bias_tail(_select_lane(cblk, hg * hps + j), query=False)
                vt_scr[j, :, pl.ds(r0, tk)] = _dg(eye, v_ref[pl.ds(r0, tk), j * hd:(j + 1) * hd], NT).astype(BF16)
            return carry
        lax.fori_loop(0, seq // tk, build, 0)

    cq_blk = c_ref[pl.ds(pl.multiple_of(qi * tq, tq), tq), :] * LOG2E
    qa = [jnp.concatenate([q_ref[:, j * hd:(j + 1) * hd],
                           _bias_tail(_select_lane(cq_blk, hg * hps + j), query=True)], axis=1)
          for j in range(hps)]
    m_scr[...] = jnp.full_like(m_scr, -jnp.inf)
    l_scr[...] = jnp.zeros_like(l_scr)
    acc_scr[...] = jnp.zeros_like(acc_scr)

    def logits(j, ks):
        return _dg(ka_scr[j, pl.ds(ks, sub), :], qa[j], NT)

    def absorb(j, st, ks, diag_offset):
        if diag_offset is not None:
            key = lax.broadcasted_iota(jnp.int32, st.shape, 0)
            qry = lax.broadcasted_iota(jnp.int32, st.shape, 1)
            st = jnp.where(key + diag_offset <= qry, st, -jnp.inf)
        m_old = m_scr[j]
        m_new = jnp.maximum(m_old, jnp.max(st, axis=0, keepdims=True))
        pt = jnp.exp2(st - m_new)
        alpha = jnp.exp2(m_old - m_new)
        l_scr[j] = alpha * l_scr[j] + jnp.sum(pt, axis=0, keepdims=True)
        acc_scr[j] = alpha * acc_scr[j] + _dot(vt_scr[j, :, pl.ds(ks, sub)], pt.astype(BF16))
        m_scr[j] = m_new

    def key_block(k0, st_cur, masked):
        units = [(j, d) for d in range(tk // sub) for j in range(hps)]
        start = lambda d: pl.multiple_of(k0 + d * sub, sub)
        for n, (j, d) in enumerate(units):
            if n + 1 < len(units):
                st_next = logits(units[n + 1][0], start(units[n + 1][1]))
            else:
                st_next = None if masked else logits(units[0][0], pl.multiple_of(k0 + tk, sub))
            absorb(j, st_cur, start(d), d * sub if masked else None)
            st_cur = st_next
        return st_cur

    def below_diagonal(kk, st):
        return key_block(pl.multiple_of(kk * tk, tk), st, False)

    st_diag = lax.fori_loop(0, qi, below_diagonal, logits(0, 0))
    key_block(pl.multiple_of(qi * tk, tk), st_diag, True)
    for j in range(hps):
        o_ref[:, j * hd:(j + 1) * hd] = jnp.transpose(acc_scr[j] / l_scr[j]).astype(o_ref.dtype)


def _fox_prompt(qkv, c, *, heads, n_seq, seq_len, tq, hps, sub):
    m, width = qkv.shape
    d = width // 3
    hd = d // heads
    nq = seq_len // tq
    ng = heads // hps
    return pl.pallas_call(
        functools.partial(_fox_prompt_kernel, hps=hps, sub=sub), grid=(n_seq, ng, nq),
        in_specs=[pl.BlockSpec((tq, hps * hd), lambda b, g, i: (b * nq + i, g)),
                  pl.BlockSpec((seq_len, hps * hd), lambda b, g, i: (b, ng + g)),
                  pl.BlockSpec((seq_len, hps * hd), lambda b, g, i: (b, 2 * ng + g)),
                  pl.BlockSpec((seq_len, 128), lambda b, g, i: (b, 0)),
                  pl.BlockSpec(memory_space=pl.ANY)],
        out_specs=pl.BlockSpec((tq, hps * hd), lambda b, g, i: (b * nq + i, g)),
        out_shape=jax.ShapeDtypeStruct((m, d), BF16), input_output_aliases={4: 0},
        scratch_shapes=[pltpu.VMEM((hps, seq_len, 2 * hd), BF16), pltpu.VMEM((hps, hd, seq_len), BF16),
                        pltpu.VMEM((hps, 1, tq), F32), pltpu.VMEM((hps, 1, tq), F32),
                        pltpu.VMEM((hps, hd, tq), F32)],
        compiler_params=_params("arbitrary", "arbitrary", "arbitrary"), name="fox_prompt")(
            qkv, qkv, qkv, c, _mixer_out(None, m, d))


def _fox_sample_kernel(q_ref, kn_ref, vn_ref, c_ref, kc_ref, vc_ref, lfc_ref, prev_ref, o_ref, d_scr, ct_scr):
    del prev_ref
    h = pl.program_id(1)
    past = kc_ref.shape[0]
    ts = q_ref.shape[0]

    @pl.when(h == 0)
    def _():
        later = (lax.broadcasted_iota(jnp.int32, (past, past), 0)
                 > lax.broadcasted_iota(jnp.int32, (past, past), 1)).astype(BF16)
        hi, mid, lo = _split3(lfc_ref[...])
        d_scr[...] = _dot(hi, later) + _dot(mid, later) + _dot(lo, later)
        ct_scr[...] = _transpose_exact(c_ref[...])

    cq = _select_lane(c_ref[...], h)
    d_row = d_scr[pl.ds(h, 1), :]
    cn_row = ct_scr[pl.ds(h, 1), :]
    q = q_ref[...]
    s_past = _dg(q, kc_ref[...].astype(BF16), NT) + (cq + d_row) * LOG2E
    s_new = _dg(q, kn_ref[...], NT) + (cq - cn_row) * LOG2E
    row = lax.broadcasted_iota(jnp.int32, (ts, ts), 0)
    col = lax.broadcasted_iota(jnp.int32, (ts, ts), 1)
    s_new = jnp.where(col <= row, s_new, -jnp.inf)
    m = jnp.maximum(jnp.max(s_past, axis=1, keepdims=True), jnp.max(s_new, axis=1, keepdims=True))
    p_past = jnp.exp2(s_past - m)
    p_new = jnp.exp2(s_new - m)
    den = jnp.sum(p_past, axis=1, keepdims=True) + jnp.sum(p_new, axis=1, keepdims=True)
    o = _dot(p_past.astype(BF16), vc_ref[...].astype(BF16)) + _dot(p_new.astype(BF16), vn_ref[...])
    o_ref[...] = (o / den).astype(o_ref.dtype)


def _fox_sample(qkv, c, k_cache, v_cache, lf_cache_t, prev_out, *, heads, row0, ts):
    m, width = qkv.shape
    d = width // 3
    hd = d // heads
    n_streams, past, _ = k_cache.shape
    base = row0 // ts
    return pl.pallas_call(
        _fox_sample_kernel, grid=(n_streams, heads),
        in_specs=[pl.BlockSpec((ts, hd), lambda b, h: (base + b, h)),
                  pl.BlockSpec((ts, hd), lambda b, h: (base + b, heads + h)),
                  pl.BlockSpec((ts, hd), lambda b, h: (base + b, 2 * heads + h)),
                  pl.BlockSpec((ts, 128), lambda b, h: (base + b, 0)),
                  pl.BlockSpec((None, past, hd), lambda b, h: (b, 0, h)),
                  pl.BlockSpec((None, past, hd), lambda b, h: (b, 0, h)),
                  pl.BlockSpec((None, heads, past), lambda b, h: (b, 0, 0)),
                  pl.BlockSpec(memory_space=pl.ANY)],
        out_specs=pl.BlockSpec((ts, hd), lambda b, h: (base + b, h)),
        out_shape=jax.ShapeDtypeStruct((m, d), BF16),
        scratch_shapes=[pltpu.VMEM((heads, past), F32), pltpu.VMEM((128, ts), F32)],
        input_output_aliases={7: 0},
        compiler_params=_params("arbitrary", "arbitrary"), name="fox_sample")(
            qkv, qkv, qkv, c, k_cache, v_cache, lf_cache_t, prev_out)


def _pad_cols(w, n):
    return jnp.pad(w, ((0, 0), (0, n - w.shape[1])))


def _row_vec(v, n=None):
    v = v.astype(F32).reshape(1, -1)
    return v if n is None else _pad_cols(v, n)


def _block_diag(blocks, copies):
    g, p, _ = blocks.shape
    eye = jnp.eye(copies, dtype=blocks.dtype)
    return jnp.einsum("ab,gpq->gapbq", eye, blocks).reshape(g, copies * p, copies * p)


def _gmlp_spatial(w_s, b_s, block, group_dim):
    pos = jnp.arange(block)
    mask = (pos[None, :] // GMLP_CHUNK) <= (pos[:, None] // GMLP_CHUNK)
    ws = jnp.where(mask[None], w_s[:, :block, :block], 0.0)
    copies = ROW_TILE // block
    bias = jnp.repeat(jnp.tile(jnp.transpose(b_s[:, :block]), (copies, 1)), group_dim, axis=1)
    return _block_diag(ws, copies).astype(BF16), bias.astype(F32)


def _cumsum_matrix(block):
    r = jnp.arange(ROW_TILE)
    return ((r[None, :] <= r[:, None]) & (r[None, :] // block == r[:, None] // block)).astype(BF16)


def kernel(x_prompt, x_sample, state_b_C, state_b_n, state_b_m, state_c_S, cache_d_k, cache_d_v, cache_d_logf, a_w_in, a_b_in, a_vn_g, a_vn_b, a_w_s, a_b_s, a_w_out, b_w_in, b_b_gates, b_norm_g, b_w_out, c_w_in, c_gn_g, c_gn_b, c_w_out, d_w_in, d_b_f, d_w_out, ffn_w_in, ffn_w_out, ln1_g, ln1_b, ln2_g, ln2_b):
    nbp, seq, d = x_prompt.shape
    nbs, ts, _ = x_sample.shape
    mp, ms = nbp * seq, nbs * ts
    past = cache_d_k.shape[2]
    depth = ffn_w_in.shape[0]
    alpha = (2.0 * depth) ** 0.25
    assert ms == ROW_TILE and ROW_TILE % ts == 0 and seq % ATTN_BLOCK == 0 and seq % SEQ_CHUNK == 0
    n_prompt_tiles = mp // ROW_TILE

    x = jnp.concatenate([x_prompt.reshape(mp, d), x_sample.reshape(ms, d)], axis=0)

    a_vs = []
    b_cp, b_np, b_mp, b_cs, b_ns, b_ms = [], [], [], [], [], []
    c_sp, c_ss = [], []
    d_kp, d_vp, d_fp, d_ks, d_vs, d_fs = [], [], [], [], [], []

    for i in range(depth):
        kind, j = i % 4, i // 4
        if kind == 0:
            w_in = a_w_in[j].astype(BF16)
            u, v, v_last = _gmlp_in(x, w_in[:, :d], _row_vec(a_b_in[j, :d]), w_in[:, d:], _row_vec(a_b_in[j, d:]),
                                    _row_vec(a_vn_g[j]), _row_vec(a_vn_b[j]))
            gd = d // GMLP_GROUPS
            wp, bp = _gmlp_spatial(a_w_s[j], a_b_s[j], GMLP_BLOCK, gd)
            wsm, bsm = _gmlp_spatial(a_w_s[j], a_b_s[j], ts, gd)
            mix = _gmlp_gate(u, v, jnp.stack([wp, wsm]), jnp.stack([bp, bsm]), n_prompt_tiles)
            w_out = a_w_out[j]
            a_vs.append(v_last[v_last.shape[0] - ms:].reshape(nbs, ts, d))
        elif kind == 1:
            heads = MLSTM_HEADS
            n_main = b_w_in.shape[2] - 2 * heads
            dqk = (n_main - 2 * d) // (2 * heads)
            qkv = _qkv_proj(x, b_w_in[j, :, :n_main].astype(BF16), n_scaled=(heads * dqk) // 1024,
                            scale=dqk ** -0.5)[0]
            gates = _mlstm_gates(x, _pad_cols(b_w_in[j, :, n_main:], 128).astype(BF16),
                                 _row_vec(b_b_gates[j], 128), heads)
            norm_g = _row_vec(b_norm_g[j])
            zc = jnp.zeros((nbp,) + state_b_C.shape[2:], F32)
            zn = jnp.zeros((nbp,) + state_b_n.shape[2:], F32)
            zm = jnp.zeros((nbp,) + state_b_m.shape[2:], F32)
            mix, cp, np_, mp_state = _mlstm(qkv, gates, norm_g, zc, zn, zm, chunk=SEQ_CHUNK, row0=0, seq_len=seq)
            mix, cs, ns, ms_state = _mlstm(qkv, gates, norm_g, state_b_C[j], state_b_n[j], state_b_m[j],
                                           chunk=ts, row0=mp, seq_len=ts, prev_out=mix)
            b_cp.append(cp); b_np.append(np_); b_mp.append(mp_state)
            b_cs.append(cs); b_ns.append(ns); b_ms.append(ms_state)
            w_out = b_w_out[j]
        elif kind == 2:
            heads = RET_HEADS
            dk = d // heads
            half = dk // 2
            inv = ROPE_BASE ** (-jnp.arange(half, dtype=F32) / half)
            pos = jnp.concatenate([jnp.tile(jnp.arange(seq), nbp), past + jnp.tile(jnp.arange(ts), nbs)])
            ang = pos.astype(F32)[:, None] * inv[None, :]
            qkvg = _rot_proj(x, c_w_in[j].astype(BF16), jnp.cos(ang), jnp.sin(ang),
                             n_q=d // 1024, n_k=d // 1024, k_scale=dk ** -0.5, head=dk)
            gn_g, gn_b = _row_vec(c_gn_g[j]), _row_vec(c_gn_b[j])
            zs = jnp.zeros((nbp,) + state_c_S.shape[2:], F32)
            mix, sp = _retention(qkvg, gn_g, gn_b, zs, chunk=SEQ_CHUNK, row0=0, seq_len=seq)
            mix, ss = _retention(qkvg, gn_g, gn_b, state_c_S[j], chunk=ts, row0=mp, seq_len=ts, prev_out=mix)
            c_sp.append(sp); c_ss.append(ss)
            w_out = c_w_out[j]
        else:
            heads = FOX_HEADS
            hd = d // heads
            qkv, k_p, k_s, v_p, v_s = _qkv_proj(x, d_w_in[j, :, :3 * d].astype(BF16), n_scaled=d // 1024,
                                                scale=hd ** -0.5 * LOG2E, kv_rows=(mp, ms))
            tri = jnp.stack([_cumsum_matrix(ROW_TILE), _cumsum_matrix(ts)])
            logf, c = _fox_gates(x, _pad_cols(d_w_in[j, :, 3 * d:], 128).astype(BF16), _row_vec(d_b_f[j], 128),
                                 tri, seq // ROW_TILE, n_prompt_tiles)
            mix = _fox_prompt(qkv, c, heads=heads, n_seq=nbp, seq_len=seq, tq=ATTN_BLOCK, hps=ATTN_HEADS_PER_STEP,
                              sub=ATTN_KEY_SUB)
            mix = _fox_sample(qkv, c, cache_d_k[j].reshape(nbs, past, d), cache_d_v[j].reshape(nbs, past, d),
                              jnp.transpose(cache_d_logf[j], (0, 2, 1)), mix, heads=heads, row0=mp, ts=ts)
            d_kp.append(k_p.reshape(nbp, seq, heads, hd)); d_ks.append(k_s.reshape(nbs, ts, heads, hd))
            d_vp.append(v_p.reshape(nbp, seq, heads, hd)); d_vs.append(v_s.reshape(nbs, ts, heads, hd))
            d_fp.append(logf[:mp, :heads].reshape(nbp, seq, heads))
            d_fs.append(logf[mp:, :heads].reshape(nbs, ts, heads))
            w_out = d_w_out[j]
        x = _out_ln(mix, w_out.astype(BF16), x, _row_vec(ln1_g[i]), _row_vec(ln1_b[i]), alpha)
        x = _ffn(x, ffn_w_in[i].astype(BF16), ffn_w_out[i].astype(BF16), _row_vec(ln2_g[i]), _row_vec(ln2_b[i]),
                 alpha, final_rows=(mp, ms) if i == depth - 1 else None)

    y_prompt, y_sample = x
    return (y_prompt.reshape(nbp, seq, d), y_sample.reshape(nbs, ts, d), jnp.stack(a_vs),
            jnp.stack(b_cp), jnp.stack(b_np), jnp.stack(b_mp),
            jnp.stack(b_cs), jnp.stack(b_ns), jnp.stack(b_ms),
            jnp.stack(c_sp), jnp.stack(c_ss),
            jnp.stack(d_kp), jnp.stack(d_vp), jnp.stack(d_fp),
            jnp.stack(d_ks), jnp.stack(d_vs), jnp.stack(d_fs))
```

```python
import functools
import math

import jax
import jax.numpy as jnp
from jax import lax
from jax.experimental import pallas as pl
from jax.experimental.pallas import tpu as pltpu

F32 = jnp.float32
BF16 = jnp.bfloat16

LN_EPS = 1e-5
ROPE_BASE = 10000.0
GMLP_BLOCK = 128
GMLP_CHUNK = 64
GMLP_GROUPS = 8
MLSTM_HEADS = 8
RET_HEADS = 8
FOX_HEADS = 16
ROW_TILE = 256
SEQ_CHUNK = 256
ATTN_BLOCK = 512
ATTN_HEADS_PER_STEP = 2
ATTN_KEY_SUB = 256
LOG2E = math.log2(math.e)
VMEM_LIMIT = 56 * 2 ** 20

NT = (((1,), (1,)), ((), ()))
TN = (((0,), (0,)), ((), ()))


def _params(*sem):
    return pltpu.CompilerParams(dimension_semantics=sem, vmem_limit_bytes=VMEM_LIMIT)


def _row_tile(m, candidates):
    for t in candidates:
        if m % t == 0:
            return t
    raise ValueError(f"no row tile for {m}")


def _dot(a, b):
    return jnp.dot(a, b, preferred_element_type=F32)


def _dg(a, b, dims):
    return lax.dot_general(a, b, dims, preferred_element_type=F32)


def _split3(x):
    hi = x.astype(BF16)
    r1 = x - hi.astype(F32)
    mid = r1.astype(BF16)
    lo = (r1 - mid.astype(F32)).astype(BF16)
    return hi, mid, lo


def _dot_exact_lhs(mat_bf16, x):
    hi, mid, lo = _split3(x)
    return _dot(mat_bf16, hi) + _dot(mat_bf16, mid) + _dot(mat_bf16, lo)


def _transpose_exact(x):
    n = x.shape[1]
    eye = (lax.broadcasted_iota(jnp.int32, (n, n), 0) == lax.broadcasted_iota(jnp.int32, (n, n), 1)).astype(BF16)
    hi, mid, lo = _split3(x)
    return _dg(eye, hi, NT) + _dg(eye, mid, NT) + _dg(eye, lo, NT)


def _sigmoid(x):
    return 1.0 / (1.0 + jnp.exp(-x))


def _log_sigmoid(x):
    return jnp.minimum(x, 0.0) - jnp.log1p(jnp.exp(-jnp.abs(x)))


def _gelu_tanh(x):
    return 0.5 * x * (1.0 + jnp.tanh(math.sqrt(2.0 / math.pi) * (x + 0.044715 * (x * x * x))))


def _layer_norm(y, g, b):
    mu = jnp.mean(y, axis=-1, keepdims=True)
    d = y - mu
    var = jnp.mean(d * d, axis=-1, keepdims=True)
    return d * lax.rsqrt(var + LN_EPS) * g + b


def _matmul_row_parts(lhs_ref, w_ref, epilogue, cast=False):
    tm = lhs_ref.shape[0]
    n_parts = 3 if tm % 48 == 0 else 2
    part = tm // n_parts
    rows = [slice(p * part, (p + 1) * part) for p in range(n_parts)]

    def matmul(r):
        lhs = lhs_ref[r, :]
        return _dot(lhs.astype(BF16) if cast else lhs, w_ref[...])

    acc = matmul(rows[0])
    for p, r in enumerate(rows):
        nxt = matmul(rows[p + 1]) if p + 1 < n_parts else None
        epilogue(r, acc)
        acc = nxt


def _cast_rows_once(x_ref, xb_scr, j):
    @pl.when(j == 0)
    def _():
        xb_scr[...] = x_ref[...].astype(BF16)


def _gelu_proj_kernel(x_ref, w_ref, b_ref, o_ref):
    def epilogue(r, acc):
        o_ref[r, :] = _gelu_tanh(acc + b_ref[...]).astype(o_ref.dtype)

    _matmul_row_parts(x_ref, w_ref, epilogue, cast=True)


def _gelu_ln_proj_kernel(x_ref, w_ref, b_ref, g_ref, be_ref, o_ref, last_ref):
    def epilogue(r, acc):
        v = _layer_norm(_gelu_tanh(acc + b_ref[...]), g_ref[...], be_ref[...])
        o_ref[r, :] = v.astype(o_ref.dtype)
        last_ref[r, :] = v

    _matmul_row_parts(x_ref, w_ref, epilogue, cast=True)


def _gmlp_in(x, w_u, b_u, w_v, b_v, vn_g, vn_b):
    m, d = x.shape
    tm = _row_tile(m, (384, 256))
    n = w_u.shape[1]
    row = pl.BlockSpec((tm, d), lambda i: (i, 0))
    wsp = pl.BlockSpec((d, n), lambda i: (0, 0))
    vec = pl.BlockSpec((1, n), lambda i: (0, 0))
    out = pl.BlockSpec((tm, n), lambda i: (i, 0))
    u = pl.pallas_call(
        _gelu_proj_kernel, grid=(m // tm,), in_specs=[row, wsp, vec], out_specs=out,
        out_shape=jax.ShapeDtypeStruct((m, n), BF16), compiler_params=_params("arbitrary"),
        name="gmlp_u")(x, w_u, b_u)
    v, v_last = pl.pallas_call(
        _gelu_ln_proj_kernel, grid=(m // tm,), in_specs=[row, wsp, vec, vec, vec],
        out_specs=[out, pl.BlockSpec((tm, n), lambda i: (0, 0))],
        out_shape=[jax.ShapeDtypeStruct((m, n), BF16), jax.ShapeDtypeStruct((tm, n), F32)],
        compiler_params=_params("arbitrary"), name="gmlp_v")(x, w_v, b_v, vn_g, vn_b)
    return u, v, v_last


def _qkv_proj_kernel(x_ref, w_ref, wg_ref, bg_ref, o_ref, gate_ref, *refs, n_scaled, scale, nb, tail, n_raw):
    kv_refs, xb_scr = refs[:-1], refs[-1]
    j = pl.program_id(1)
    _cast_rows_once(x_ref, xb_scr, j)

    @pl.when(j == 0)
    def _():
        g = _dot(xb_scr[...], wg_ref[...]) + bg_ref[...]
        lane = lax.broadcasted_iota(jnp.int32, g.shape, 1)
        gate_ref[...] = jnp.where(lane < n_raw, g, _log_sigmoid(g))

    col_scale = jnp.where(j < n_scaled, scale, 1.0).astype(F32)
    accs = []

    def epilogue(r, acc):
        o_ref[r, :] = (acc * col_scale).astype(o_ref.dtype)
        accs.append(acc)

    _matmul_row_parts(xb_scr, w_ref, epilogue)

    if kv_refs:
        kp_ref, ks_ref, vp_ref, vs_ref = kv_refs
        tm, tn = o_ref.shape
        acc = jnp.concatenate(accs, axis=0)

        @pl.when(jnp.logical_and(j >= nb, j < 2 * nb))
        def _():
            kp_ref[...] = acc
            ks_ref[:, pl.ds(pl.multiple_of((j - nb) * tn, tn), tn)] = acc[tm - tail:]

        @pl.when(j >= 2 * nb)
        def _():
            vp_ref[...] = acc
            vs_ref[:, pl.ds(pl.multiple_of((j - 2 * nb) * tn, tn), tn)] = acc[tm - tail:]


def _qkv_proj(x, w, w_gate, b_gate, *, n_scaled, scale, n_raw, kv_rows=None, tn=1024):
    m, d = x.shape
    n = w.shape[1]
    tm = _row_tile(m, (768, 512, 256))
    in_specs = [pl.BlockSpec((tm, d), lambda i, j: (i, 0)), pl.BlockSpec((d, tn), lambda i, j: (0, j)),
                pl.BlockSpec((d, 128), lambda i, j: (0, 0)), pl.BlockSpec((1, 128), lambda i, j: (0, 0))]
    out_specs = [pl.BlockSpec((tm, tn), lambda i, j: (i, j)), pl.BlockSpec((tm, 128), lambda i, j: (i, 0))]
    out_shape = [jax.ShapeDtypeStruct((m, n), BF16), jax.ShapeDtypeStruct((m, 128), F32)]
    nb, tail = n // (3 * tn), 0
    if kv_rows is not None:
        mp, tail = kv_rows
        assert mp + tail == m and tail <= tm
        for first in (nb, 2 * nb):
            col = functools.partial(lambda j, first: jnp.clip(j - first, 0, nb - 1), first=first)
            out_specs += [pl.BlockSpec((tm, tn), lambda i, j, col=col: (i, col(j))),
                          pl.BlockSpec((tail, n // 3), lambda i, j: (0, 0))]
            out_shape += [jax.ShapeDtypeStruct((mp, n // 3), F32), jax.ShapeDtypeStruct((tail, n // 3), F32)]
    return pl.pallas_call(
        functools.partial(_qkv_proj_kernel, n_scaled=n_scaled, scale=scale, nb=nb, tail=tail, n_raw=n_raw),
        grid=(m // tm, n // tn), in_specs=in_specs, out_specs=out_specs, out_shape=out_shape,
        scratch_shapes=[pltpu.VMEM((tm, d), BF16)],
        compiler_params=_params("arbitrary", "arbitrary"), name="qkv_proj")(x, w, w_gate, b_gate)


def _rot_proj_kernel(x_ref, w_ref, cos_ref, sin_ref, o_ref, xb_scr, *, n_q, n_k, k_scale, head):
    j = pl.program_id(1)
    _cast_rows_once(x_ref, xb_scr, j)
    half = head // 2
    rotated = j < n_q + n_k
    col_scale = jnp.where(jnp.logical_and(j >= n_q, rotated), k_scale, 1.0).astype(F32)

    def epilogue(r, acc):
        cos = jnp.where(rotated, cos_ref[r, :], 1.0)
        sin = jnp.where(rotated, sin_ref[r, :], 0.0)
        for h in range(acc.shape[1] // head):
            x1 = acc[:, h * head:h * head + half]
            x2 = acc[:, h * head + half:(h + 1) * head]
            o_ref[r, h * head:h * head + half] = ((x1 * cos - x2 * sin) * col_scale).astype(o_ref.dtype)
            o_ref[r, h * head + half:(h + 1) * head] = ((x1 * sin + x2 * cos) * col_scale).astype(o_ref.dtype)

    _matmul_row_parts(xb_scr, w_ref, epilogue)


def _rot_proj(x, w, cos, sin, *, n_q, n_k, k_scale, head, tn=1024):
    m, d = x.shape
    n = w.shape[1]
    tm = _row_tile(m, (768, 512, 256))
    half = head // 2
    return pl.pallas_call(
        functools.partial(_rot_proj_kernel, n_q=n_q, n_k=n_k, k_scale=k_scale, head=head),
        grid=(m // tm, n // tn),
        in_specs=[pl.BlockSpec((tm, d), lambda i, j: (i, 0)), pl.BlockSpec((d, tn), lambda i, j: (0, j)),
                  pl.BlockSpec((tm, half), lambda i, j: (i, 0)), pl.BlockSpec((tm, half), lambda i, j: (i, 0))],
        out_specs=pl.BlockSpec((tm, tn), lambda i, j: (i, j)),
        out_shape=jax.ShapeDtypeStruct((m, n), BF16), scratch_shapes=[pltpu.VMEM((tm, d), BF16)],
        compiler_params=_params("arbitrary", "arbitrary"), name="rot_proj")(x, w, cos, sin)


def _running_sum_kernel(lf_ref, tri_ref, c_ref, carry_ref, *, tiles_per_seq):
    i = pl.program_id(0)

    @pl.when(i % tiles_per_seq == 0)
    def _():
        carry_ref[...] = jnp.zeros_like(carry_ref)

    c = _dot_exact_lhs(tri_ref[...], lf_ref[...]) + carry_ref[...]
    c_ref[...] = c
    carry_ref[...] = c[-1:, :]


def _running_sum(lf, tri, tiles_per_seq, n_prompt_tiles):
    m = lf.shape[0]
    tm = ROW_TILE
    blk = pl.BlockSpec((tm, 128), lambda i: (i, 0))
    return pl.pallas_call(
        functools.partial(_running_sum_kernel, tiles_per_seq=tiles_per_seq), grid=(m // tm,),
        in_specs=[blk, pl.BlockSpec((None, tm, tm), lambda i: (i // n_prompt_tiles, 0, 0))],
        out_specs=blk, out_shape=jax.ShapeDtypeStruct((m, 128), F32),
        scratch_shapes=[pltpu.VMEM((1, 128), F32)],
        compiler_params=_params("arbitrary"), name="running_sum")(lf, tri)


def _out_ln_kernel(h_ref, w_ref, x_ref, g_ref, b_ref, o_ref, *, alpha):
    def epilogue(r, acc):
        o_ref[r, :] = _layer_norm(alpha * x_ref[r, :] + acc, g_ref[...], b_ref[...])

    _matmul_row_parts(h_ref, w_ref, epilogue)


def _out_ln(h, w, x, g, b, alpha):
    m, k = h.shape
    d = w.shape[1]
    tm = _row_tile(m, (384, 256))
    vec = pl.BlockSpec((1, d), lambda i: (0, 0))
    row = pl.BlockSpec((tm, d), lambda i: (i, 0))
    return pl.pallas_call(
        functools.partial(_out_ln_kernel, alpha=alpha), grid=(m // tm,),
        in_specs=[pl.BlockSpec((tm, k), lambda i: (i, 0)),
                  pl.BlockSpec((k, d), lambda i: (0, 0), pipeline_mode=pl.Buffered(1)),
                  row, vec, vec],
        out_specs=row, out_shape=jax.ShapeDtypeStruct((m, d), F32),
        compiler_params=_params("arbitrary"), name="out_ln")(h, w, x, g, b)


def _ffn_kernel(x_ref, wg_ref, wu_ref, wo_ref, g_ref, b_ref, o_ref, *rest, alpha):
    tail_refs, xb_scr = rest[:-1], rest[-1]
    j = pl.program_id(1)
    _cast_rows_once(x_ref, xb_scr, j)

    @pl.when(j == 0)
    def _():
        o_ref[...] = jnp.zeros_like(o_ref)

    xb = xb_scr[...]
    gate = _dot(xb, wg_ref[...])
    up = _dot(xb, wu_ref[...])
    hid = (gate * _sigmoid(gate) * up).astype(BF16)
    o_ref[...] += _dot(hid, wo_ref[...])

    @pl.when(j == pl.num_programs(1) - 1)
    def _():
        y = _layer_norm(alpha * x_ref[...] + o_ref[...], g_ref[...], b_ref[...])
        o_ref[...] = y
        for t_ref in tail_refs:
            t_ref[...] = y[y.shape[0] - t_ref.shape[0]:]


def _ffn(x, w_in, w_out, layer, g, b, alpha, final_rows=None, th=512):
    m, d = x.shape
    hidden = w_out.shape[1]
    tm = _row_tile(m, (768, 512, 256))
    nh = hidden // th
    vec = pl.BlockSpec((1, d), lambda i, j: (0, 0))
    row = pl.BlockSpec((tm, d), lambda i, j: (i, 0))
    if final_rows is None:
        out_specs, out_shape = row, jax.ShapeDtypeStruct((m, d), F32)
    else:
        mp, tail = final_rows
        assert mp + tail == m and tail <= tm
        out_specs = [row, pl.BlockSpec((tail, d), lambda i, j: (0, 0))]
        out_shape = [jax.ShapeDtypeStruct((mp, d), F32), jax.ShapeDtypeStruct((tail, d), F32)]
    return pl.pallas_call(
        functools.partial(_ffn_kernel, alpha=alpha), grid=(m // tm, nh),
        in_specs=[row, pl.BlockSpec((None, d, th), lambda i, j: (layer, 0, j)),
                  pl.BlockSpec((None, d, th), lambda i, j: (layer, 0, j + nh)),
                  pl.BlockSpec((None, th, d), lambda i, j: (layer, j, 0)), vec, vec],
        out_specs=out_specs, out_shape=out_shape,
        scratch_shapes=[pltpu.VMEM((tm, d), BF16)],
        compiler_params=_params("arbitrary", "arbitrary"), name="ffn")(x, w_in, w_in, w_out, g, b)


def _gmlp_gate_kernel(u_ref, v_ref, w_ref, b_ref, y_ref, *, groups):
    gd = u_ref.shape[1] // groups
    for g in range(groups):
        sl = slice(g * gd, (g + 1) * gd)
        s = _dot(w_ref[g], v_ref[:, sl]) + b_ref[:, sl]
        y_ref[:, sl] = (u_ref[:, sl].astype(F32) * s).astype(y_ref.dtype)


def _gmlp_gate(u, v, w_sp, b_sp, n_prompt_tiles):
    m, d = u.shape
    tm = ROW_TILE
    groups = w_sp.shape[1]
    row = pl.BlockSpec((tm, d), lambda i: (i, 0))
    return pl.pallas_call(
        functools.partial(_gmlp_gate_kernel, groups=groups), grid=(m // tm,),
        in_specs=[row, row,
                  pl.BlockSpec((None, groups, tm, tm), lambda i: (i // n_prompt_tiles, 0, 0, 0)),
                  pl.BlockSpec((None, tm, d), lambda i: (i // n_prompt_tiles, 0, 0))],
        out_specs=row, out_shape=jax.ShapeDtypeStruct((m, d), BF16),
        compiler_params=_params("arbitrary"), name="gmlp_gate")(u, v, w_sp, b_sp)


def _mixer_out(prev_out, rows, cols):
    return jnp.zeros((rows, cols), BF16) if prev_out is None else prev_out


def _mlstm_kernel(q_ref, k_ref, v_ref, og_ref, gt_ref, ng_ref, c0_ref, n0_ref, m0_ref, *refs, heads):
    h_ref, c_ref, n_ref, m_ref = refs[-4:]
    step = pl.program_id(1)
    L = q_ref.shape[0]
    dqk = q_ref.shape[1] // heads
    dv = v_ref.shape[1] // heads

    @pl.when(step == 0)
    def _():
        c_ref[...] = c0_ref[...]
        n_ref[...] = n0_ref[...]
        m_ref[...] = m0_ref[...]

    g = gt_ref[...]
    row = lax.broadcasted_iota(jnp.int32, (L, L), 0)
    col = lax.broadcasted_iota(jnp.int32, (L, L), 1)
    causal = col <= row
    bc = _dot_exact_lhs(causal.astype(BF16), g)
    g_t = _transpose_exact(g)
    bc_t = _transpose_exact(bc)

    for h in range(heads):
        a = bc[:, heads + h:heads + h + 1]
        li = g[:, h:h + 1]
        a_row = bc_t[heads + h:heads + h + 1, :]
        li_row = g_t[h:h + 1, :]
        m_prev = m_ref[h:h + 1, :]
        dmat = jnp.where(causal, a - a_row + li_row, -jnp.inf)
        inter = a + m_prev
        m_t = jnp.maximum(inter, jnp.max(dmat, axis=1, keepdims=True))
        w_intra = jnp.exp(dmat - m_t)
        w_inter = jnp.exp(inter - m_t)
        qh = q_ref[:, h * dqk:(h + 1) * dqk]
        kh = k_ref[:, h * dqk:(h + 1) * dqk]
        vh = v_ref[:, h * dv:(h + 1) * dv]
        c_old = c_ref[h]
        n_old = n_ref[h:h + 1, :]
        scores = _dg(qh, kh, NT) * w_intra
        num = _dot(scores.astype(BF16), vh) + w_inter * _dg(qh, c_old.astype(BF16), NT)
        den = (jnp.sum(scores, axis=1, keepdims=True)
               + w_inter * jnp.sum(qh.astype(F32) * n_old, axis=1, keepdims=True))
        hid = num / jnp.maximum(jnp.abs(den), jnp.exp(-m_t))
        sl = slice(h * dv, (h + 1) * dv)
        y = hid * lax.rsqrt(jnp.mean(hid * hid, axis=1, keepdims=True) + LN_EPS) * ng_ref[:, sl]
        h_ref[:, sl] = (y * _sigmoid(og_ref[:, sl].astype(F32))).astype(h_ref.dtype)

        m_new = m_t[L - 1:L, :]
        b_last = a[L - 1:L, :]
        w_c = jnp.exp(b_last + m_prev - m_new)
        w_s = jnp.exp(b_last - a + li - m_new)
        c_ref[h] = w_c * c_old + _dg((vh.astype(F32) * w_s).astype(BF16), kh, TN)
        n_ref[h:h + 1, :] = w_c * n_old + jnp.sum(kh.astype(F32) * w_s, axis=0, keepdims=True)
        m_ref[h:h + 1, :] = m_new


def _mlstm(qkv, gates, norm_g, c0, n0, m0, *, chunk, row0, seq_len, prev_out=None):
    m, width = qkv.shape
    d = width // 3
    nb, heads, dv, dqk = c0.shape
    nc = seq_len // chunk
    base = row0 // chunk

    def rb(b, c):
        return base + b * nc + c

    in_specs = [pl.BlockSpec((chunk, d // 2), lambda b, c: (rb(b, c), 0)),
                pl.BlockSpec((chunk, d // 2), lambda b, c: (rb(b, c), 1)),
                pl.BlockSpec((chunk, d), lambda b, c: (rb(b, c), 1)),
                pl.BlockSpec((chunk, d), lambda b, c: (rb(b, c), 2)),
                pl.BlockSpec((chunk, 128), lambda b, c: (rb(b, c), 0)),
                pl.BlockSpec((1, d), lambda b, c: (0, 0)),
                pl.BlockSpec((None, heads, dv, dqk), lambda b, c: (b, 0, 0, 0)),
                pl.BlockSpec((None, heads, dqk), lambda b, c: (b, 0, 0)),
                pl.BlockSpec((None, heads, 1), lambda b, c: (b, 0, 0))]
    args = [qkv, qkv, qkv, qkv, gates, norm_g, c0, n0, m0.reshape(nb, heads, 1)]
    in_specs.append(pl.BlockSpec(memory_space=pl.ANY))
    args.append(_mixer_out(prev_out, m, d))
    aliases = {len(args) - 1: 0}
    outs = pl.pallas_call(
        functools.partial(_mlstm_kernel, heads=heads), grid=(nb, nc), in_specs=in_specs,
        out_specs=[pl.BlockSpec((chunk, d), lambda b, c: (rb(b, c), 0)),
                   pl.BlockSpec((None, heads, dv, dqk), lambda b, c: (b, 0, 0, 0)),
                   pl.BlockSpec((None, heads, dqk), lambda b, c: (b, 0, 0)),
                   pl.BlockSpec((None, heads, 1), lambda b, c: (b, 0, 0))],
        out_shape=[jax.ShapeDtypeStruct((m, d), BF16), jax.ShapeDtypeStruct(c0.shape, F32),
                   jax.ShapeDtypeStruct(n0.shape, F32), jax.ShapeDtypeStruct((nb, heads, 1), F32)],
        input_output_aliases=aliases,
        compiler_params=_params("arbitrary", "arbitrary"), name="mlstm")(*args)
    return outs[0], outs[1], outs[2], outs[3].reshape(nb, heads)


def _retention_kernel(q_ref, k_ref, v_ref, g_ref, gg_ref, gb_ref, s0_ref, *refs, heads):
    o_ref, s_ref = refs[-2:]
    step = pl.program_id(1)
    L = q_ref.shape[0]
    dk = q_ref.shape[1] // heads
    dv = v_ref.shape[1] // heads

    @pl.when(step == 0)
    def _():
        s_ref[...] = s0_ref[...]

    row = lax.broadcasted_iota(jnp.int32, (L, L), 0)
    col = lax.broadcasted_iota(jnp.int32, (L, L), 1)
    causal = col <= row
    lag = (row - col).astype(F32)
    t = lax.broadcasted_iota(jnp.int32, (L, 1), 0).astype(F32)

    for h in range(heads):
        log_gamma = math.log1p(-(2.0 ** (-5.0 - h)))
        decay = jnp.where(causal, jnp.exp(lag * log_gamma), 0.0)
        decay_q = jnp.exp((t + 1.0) * log_gamma)
        decay_k = jnp.exp((L - 1.0 - t) * log_gamma)
        decay_s = math.exp(L * log_gamma)
        qh = q_ref[:, h * dk:(h + 1) * dk]
        kh = k_ref[:, h * dk:(h + 1) * dk]
        vh = v_ref[:, h * dv:(h + 1) * dv]
        s_old = s_ref[h]
        scores = _dg(qh, kh, NT) * decay
        o = _dot(scores.astype(BF16), vh) + _dot(qh, s_old.astype(BF16)) * decay_q
        s_ref[h] = decay_s * s_old + _dg((kh.astype(F32) * decay_k).astype(BF16), vh, TN)
        sl = slice(h * dv, (h + 1) * dv)
        y = _layer_norm(o, gg_ref[:, sl], gb_ref[:, sl])
        gate = g_ref[:, sl].astype(F32)
        o_ref[:, sl] = (gate * _sigmoid(gate) * y).astype(o_ref.dtype)


def _retention(qkvg, gn_g, gn_b, s0, *, chunk, row0, seq_len, prev_out=None):
    m, width = qkvg.shape
    d = width // 6
    nb, heads, dk, dv = s0.shape
    nc = seq_len // chunk
    base = row0 // chunk

    def rb(b, c):
        return base + b * nc + c

    in_specs = [pl.BlockSpec((chunk, d), lambda b, c: (rb(b, c), 0)),
                pl.BlockSpec((chunk, d), lambda b, c: (rb(b, c), 1)),
                pl.BlockSpec((chunk, 2 * d), lambda b, c: (rb(b, c), 1)),
                pl.BlockSpec((chunk, 2 * d), lambda b, c: (rb(b, c), 2)),
                pl.BlockSpec((1, 2 * d), lambda b, c: (0, 0)),
                pl.BlockSpec((1, 2 * d), lambda b, c: (0, 0)),
                pl.BlockSpec((None, heads, dk, dv), lambda b, c: (b, 0, 0, 0))]
    args = [qkvg, qkvg, qkvg, qkvg, gn_g, gn_b, s0]
    in_specs.append(pl.BlockSpec(memory_space=pl.ANY))
    args.append(_mixer_out(prev_out, m, 2 * d))
    aliases = {len(args) - 1: 0}
    return pl.pallas_call(
        functools.partial(_retention_kernel, heads=heads), grid=(nb, nc), in_specs=in_specs,
        out_specs=[pl.BlockSpec((chunk, 2 * d), lambda b, c: (rb(b, c), 0)),
                   pl.BlockSpec((None, heads, dk, dv), lambda b, c: (b, 0, 0, 0))],
        out_shape=[jax.ShapeDtypeStruct((m, 2 * d), BF16), jax.ShapeDtypeStruct(s0.shape, F32)],
        input_output_aliases=aliases,
        compiler_params=_params("arbitrary", "arbitrary"), name="retention")(*args)


def _select_lane(block, lane_index):
    lane = lax.broadcasted_iota(jnp.int32, block.shape, 1)
    return jnp.sum(jnp.where(lane == lane_index, block, 0.0), axis=1, keepdims=True)


def _bias_tail(c_col, *, query):
    hi, mid, lo = (t.astype(F32) for t in _split3(c_col))
    lane = lax.broadcasted_iota(jnp.int32, (c_col.shape[0], 128), 1)
    if query:
        tail = jnp.where(lane == 0, hi, jnp.where(lane == 1, mid, jnp.where(lane == 2, lo,
                                                                             jnp.where(lane < 6, 1.0, 0.0))))
    else:
        tail = jnp.where(lane < 3, 1.0, jnp.where(lane == 3, -hi, jnp.where(lane == 4, -mid,
                                                                             jnp.where(lane == 5, -lo, 0.0))))
    return tail.astype(BF16)


def _fox_prompt_kernel(q_ref, k_ref, v_ref, c_ref, buf_ref, o_ref, ka_scr, vt_scr, m_scr, l_scr, acc_scr, *,
                       hps, sub):
    del buf_ref
    hg = pl.program_id(1)
    qi = pl.program_id(2)
    tq = tk = q_ref.shape[0]
    seq = k_ref.shape[0]
    hd = q_ref.shape[1] // hps

    @pl.when(qi == 0)
    def _():
        eye =(lax.broadcasted_iota(jnp.int32, (hd, hd), 0)
               == lax.broadcasted_iota(jnp.int32, (hd, hd), 1)).astype(BF16)

        def build(r, carry):
            r0 = pl.multiple_of(r * tk, tk)
            cblk = c_ref[pl.ds(r0, tk), :] * LOG2E
            for j in range(hps):
                ka_scr[j, pl.ds(r0, tk), 0:hd] = k_ref[pl.ds(r0, tk), j * hd:(j + 1) * hd]
                ka_scr[j, pl.ds(r0, tk), hd:2 * hd] = _bias_tail(_select_lane(cblk, hg * hps + j), query=False)
                vt_scr[j, :, pl.ds(r0, tk)] = _dg(eye, v_ref[pl.ds(r0, tk), j * hd:(j + 1) * hd], NT).astype(BF16)
            return carry
        lax.fori_loop(0, seq // tk, build, 0)

    cq_blk = c_ref[pl.ds(pl.multiple_of(qi * tq, tq), tq), :] * LOG2E
    qa = [jnp.concatenate([q_ref[:, j * hd:(j + 1) * hd],
                           _bias_tail(_select_lane(cq_blk, hg * hps + j), query=True)], axis=1)
          for j in range(hps)]
    m_scr[...] = jnp.full_like(m_scr, -jnp.inf)
    l_scr[...] = jnp.zeros_like(l_scr)
    acc_scr[...] = jnp.zeros_like(acc_scr)

    def logits(j, ks):
        return _dg(ka_scr[j, pl.ds(ks, sub), :], qa[j], NT)

    def absorb(j, st, ks, diag_offset):
        if diag_offset is not None:
            key = lax.broadcasted_iota(jnp.int32, st.shape, 0)
            qry = lax.broadcasted_iota(jnp.int32, st.shape, 1)
            st = jnp.where(key + diag_offset <= qry, st, -jnp.inf)
        m_old = m_scr[j]
        m_new = jnp.maximum(m_old, jnp.max(st, axis=0, keepdims=True))
        pt = jnp.exp2(st - m_new)
        alpha = jnp.exp2(m_old - m_new)
        l_scr[j] = alpha * l_scr[j] + jnp.sum(pt, axis=0, keepdims=True)
        acc_scr[j] = alpha * acc_scr[j] + _dot(vt_scr[j, :, pl.ds(ks, sub)], pt.astype(BF16))
        m_scr[j] = m_new

    def key_block(k0, st_cur, masked):
        units = [(j, d) for d in range(tk // sub) for j in range(hps)]
        start = lambda d: pl.multiple_of(k0 + d * sub, sub)
        for n, (j, d) in enumerate(units):
            if n + 1 < len(units):
                st_next = logits(units[n + 1][0], start(units[n + 1][1]))
            else:
                st_next = None if masked else logits(units[0][0], pl.multiple_of(k0 + tk, sub))
            absorb(j, st_cur, start(d), d * sub if masked else None)
            st_cur = st_next
        return st_cur

    def below_diagonal(kk, st):
        return key_block(pl.multiple_of(kk * tk, tk), st, False)

    st_diag = lax.fori_loop(0, qi, below_diagonal, logits(0, 0))
    key_block(pl.multiple_of(qi * tk, tk), st_diag, True)
    for j in range(hps):
        o_ref[:, j * hd:(j + 1) * hd] = jnp.transpose(acc_scr[j] / l_scr[j]).astype(o_ref.dtype)


def _fox_prompt(qkv, c, *, heads, n_seq, seq_len, tq, hps, sub):
    m, width = qkv.shape
    d = width // 3
    hd = d // heads
    nq = seq_len // tq
    ng = heads // hps
    return pl.pallas_call(
        functools.partial(_fox_prompt_kernel, hps=hps, sub=sub), grid=(n_seq, ng, nq),
        in_specs=[pl.BlockSpec((tq, hps * hd), lambda b, g, i: (b * nq + i, g)),
                  pl.BlockSpec((seq_len, hps * hd), lambda b, g, i: (b, ng + g)),
                  pl.BlockSpec((seq_len, hps * hd), lambda b, g, i: (b, 2 * ng + g)),
                  pl.BlockSpec((seq_len, 128), lambda b, g, i: (b, 0)),
                  pl.BlockSpec(memory_space=pl.ANY)],
        out_specs=pl.BlockSpec((tq, hps * hd), lambda b, g, i: (b * nq + i, g)),
        out_shape=jax.ShapeDtypeStruct((m, d), BF16), input_output_aliases={4: 0},
        scratch_shapes=[pltpu.VMEM((hps, seq_len, 2 * hd), BF16), pltpu.VMEM((hps, hd, seq_len), BF16),
                        pltpu.VMEM((hps, 1, tq), F32), pltpu.VMEM((hps, 1, tq), F32),
                        pltpu.VMEM((hps, hd, tq), F32)],
        compiler_params=_params("arbitrary", "arbitrary", "arbitrary"), name="fox_prompt")(
            qkv, qkv, qkv, c, _mixer_out(None, m, d))


def _fox_sample_kernel(q_ref, kn_ref, vn_ref, c_ref, kc_ref, vc_ref, lfc_ref, prev_ref, o_ref, d_scr, ct_scr):
    del prev_ref
    g = pl.program_id(1)
    past, hpg, hd = kc_ref.shape
    ts = q_ref.shape[0]

    @pl.when(g == 0)
    def _():
        later = (lax.broadcasted_iota(jnp.int32, (past, past), 0)
                 > lax.broadcasted_iota(jnp.int32, (past, past), 1)).astype(BF16)
        hi, mid, lo = _split3(lfc_ref[...])
        d_scr[...] = _dot(hi, later) + _dot(mid, later) + _dot(lo, later)
        ct_scr[...] = _transpose_exact(c_ref[...])

    c_blk = c_ref[...]
    row = lax.broadcasted_iota(jnp.int32, (ts, ts), 0)
    col = lax.broadcasted_iota(jnp.int32, (ts, ts), 1)
    for j in range(hpg):
        h = g * hpg + j
        cq = _select_lane(c_blk, h)
        d_row = d_scr[pl.ds(h, 1), :]
        cn_row = ct_scr[pl.ds(h, 1), :]
        sl = slice(j * hd, (j + 1) * hd)
        q = q_ref[:, sl]
        s_past = _dg(q, kc_ref[:, j, :].astype(BF16), NT) + (cq + d_row) * LOG2E
        s_new = _dg(q, kn_ref[:, sl], NT) + (cq - cn_row) * LOG2E
        s_new = jnp.where(col <= row, s_new, -jnp.inf)
        m = jnp.maximum(jnp.max(s_past, axis=1, keepdims=True), jnp.max(s_new, axis=1, keepdims=True))
        p_past = jnp.exp2(s_past - m)
        p_new = jnp.exp2(s_new - m)
        den = jnp.sum(p_past, axis=1, keepdims=True) + jnp.sum(p_new, axis=1, keepdims=True)
        o = _dot(p_past.astype(BF16), vc_ref[:, j, :].astype(BF16)) + _dot(p_new.astype(BF16), vn_ref[:, sl])
        o_ref[:, sl] = (o / den).astype(o_ref.dtype)


def _fox_sample(qkv, c, k_cache, v_cache, lf_cache_t, prev_out, *, heads, row0, ts, hpg=8):
    m, width = qkv.shape
    d = width // 3
    hd = d // heads
    n_streams, past = k_cache.shape[:2]
    base = row0 // ts
    ng = heads // hpg
    return pl.pallas_call(
        _fox_sample_kernel, grid=(n_streams, ng),
        in_specs=[pl.BlockSpec((ts, hpg * hd), lambda b, g: (base + b, g)),
                  pl.BlockSpec((ts, hpg * hd), lambda b, g: (base + b, ng + g)),
                  pl.BlockSpec((ts, hpg * hd), lambda b, g: (base + b, 2 * ng + g)),
                  pl.BlockSpec((ts, 128), lambda b, g: (base + b, 0)),
                  pl.BlockSpec((None, past, hpg, hd), lambda b, g: (b, 0, g, 0)),
                  pl.BlockSpec((None, past, hpg, hd), lambda b, g: (b, 0, g, 0)),
                  pl.BlockSpec((None, heads, past), lambda b, g: (b, 0, 0)),
                  pl.BlockSpec(memory_space=pl.ANY)],
        out_specs=pl.BlockSpec((ts, hpg * hd), lambda b, g: (base + b, g)),
        out_shape=jax.ShapeDtypeStruct((m, d), BF16),
        scratch_shapes=[pltpu.VMEM((heads, past), F32), pltpu.VMEM((128, ts), F32)],
        input_output_aliases={7: 0},
        compiler_params=_params("arbitrary", "arbitrary"), name="fox_sample")(
            qkv, qkv, qkv, c, k_cache, v_cache, lf_cache_t, prev_out)


def _pad_cols(w, n):
    return jnp.pad(w, ((0, 0), (0, n - w.shape[1])))


def _row_vec(v, n=None):
    v = v.astype(F32).reshape(1, -1)
    return v if n is None else _pad_cols(v, n)


def _block_diag(blocks, copies):
    g, p, _ = blocks.shape
    eye = jnp.eye(copies, dtype=blocks.dtype)
    return jnp.einsum("ab,gpq->gapbq", eye, blocks).reshape(g, copies * p, copies * p)


def _gmlp_spatial(w_s, b_s, block, group_dim):
    pos = jnp.arange(block)
    mask = (pos[None, :] // GMLP_CHUNK) <= (pos[:, None] // GMLP_CHUNK)
    ws = jnp.where(mask[None], w_s[:, :block, :block], 0.0)
    copies = ROW_TILE // block
    bias = jnp.repeat(jnp.tile(jnp.transpose(b_s[:, :block]), (copies, 1)), group_dim, axis=1)
    return _block_diag(ws, copies).astype(BF16), bias.astype(F32)


def _cumsum_matrix(block):
    r = jnp.arange(ROW_TILE)
    return ((r[None, :] <= r[:, None]) & (r[None, :] // block == r[:, None] // block)).astype(BF16)


def kernel(x_prompt, x_sample, state_b_C, state_b_n, state_b_m, state_c_S, cache_d_k, cache_d_v, cache_d_logf, a_w_in, a_b_in, a_vn_g, a_vn_b, a_w_s, a_b_s, a_w_out, b_w_in, b_b_gates, b_norm_g, b_w_out, c_w_in, c_gn_g, c_gn_b, c_w_out, d_w_in, d_b_f, d_w_out, ffn_w_in, ffn_w_out, ln1_g, ln1_b, ln2_g, ln2_b):
    nbp, seq, d = x_prompt.shape
    nbs, ts, _ = x_sample.shape
    mp, ms = nbp * seq, nbs * ts
    past = cache_d_k.shape[2]
    depth = ffn_w_in.shape[0]
    alpha = (2.0 * depth) ** 0.25
    assert ms == ROW_TILE and ROW_TILE % ts == 0 and seq % ATTN_BLOCK == 0 and seq % SEQ_CHUNK == 0
    n_prompt_tiles = mp // ROW_TILE

    x = jnp.concatenate([x_prompt.reshape(mp, d), x_sample.reshape(ms, d)], axis=0)
    ffn_w_in_b, ffn_w_out_b = ffn_w_in.astype(BF16), ffn_w_out.astype(BF16)

    a_vs = []
    b_cp, b_np, b_mp, b_cs, b_ns, b_ms = [], [], [], [], [], []
    c_sp, c_ss = [], []
    d_kp, d_vp, d_fp, d_ks, d_vs, d_fs = [], [], [], [], [], []

    for i in range(depth):
        kind, j = i % 4, i // 4
        if kind == 0:
            w_in = a_w_in[j].astype(BF16)
            u, v, v_last = _gmlp_in(x, w_in[:, :d], _row_vec(a_b_in[j, :d]), w_in[:, d:], _row_vec(a_b_in[j, d:]),
                                    _row_vec(a_vn_g[j]), _row_vec(a_vn_b[j]))
            gd = d // GMLP_GROUPS
            wp, bp = _gmlp_spatial(a_w_s[j], a_b_s[j], GMLP_BLOCK, gd)
            wsm, bsm = _gmlp_spatial(a_w_s[j], a_b_s[j], ts, gd)
            mix = _gmlp_gate(u, v, jnp.stack([wp, wsm]), jnp.stack([bp, bsm]), n_prompt_tiles)
            w_out = a_w_out[j]
            a_vs.append(v_last[v_last.shape[0] - ms:].reshape(nbs, ts, d))
        elif kind == 1:
            heads = MLSTM_HEADS
            n_main = b_w_in.shape[2] - 2 * heads
            dqk = (n_main - 2 * d) // (2 * heads)
            qkv, gates = _qkv_proj(x, b_w_in[j, :, :n_main].astype(BF16),
                                   _pad_cols(b_w_in[j, :, n_main:], 128).astype(BF16), _row_vec(b_b_gates[j], 128),
                                   n_scaled=(heads * dqk) // 1024, scale=dqk ** -0.5, n_raw=heads)
            norm_g = _row_vec(b_norm_g[j])
            zc = jnp.zeros((nbp,) + state_b_C.shape[2:], F32)
            zn = jnp.zeros((nbp,) + state_b_n.shape[2:], F32)
            zm = jnp.zeros((nbp,) + state_b_m.shape[2:], F32)
            mix, cp, np_, mp_state = _mlstm(qkv, gates, norm_g, zc, zn, zm, chunk=SEQ_CHUNK, row0=0, seq_len=seq)
            mix, cs, ns, ms_state = _mlstm(qkv, gates, norm_g, state_b_C[j], state_b_n[j], state_b_m[j],
                                           chunk=ts, row0=mp, seq_len=ts, prev_out=mix)
            b_cp.append(cp); b_np.append(np_); b_mp.append(mp_state)
            b_cs.append(cs); b_ns.append(ns); b_ms.append(ms_state)
            w_out = b_w_out[j]
        elif kind == 2:
            heads = RET_HEADS
            dk = d // heads
            half = dk // 2
            inv = ROPE_BASE ** (-jnp.arange(half, dtype=F32) / half)
            pos = jnp.concatenate([jnp.tile(jnp.arange(seq), nbp), past + jnp.tile(jnp.arange(ts), nbs)])
            ang = pos.astype(F32)[:, None] * inv[None, :]
            qkvg = _rot_proj(x, c_w_in[j].astype(BF16), jnp.cos(ang), jnp.sin(ang),
                             n_q=d // 1024, n_k=d // 1024, k_scale=dk ** -0.5, head=dk)
            gn_g, gn_b = _row_vec(c_gn_g[j]), _row_vec(c_gn_b[j])
            zs = jnp.zeros((nbp,) + state_c_S.shape[2:], F32)
            mix, sp = _retention(qkvg, gn_g, gn_b, zs, chunk=SEQ_CHUNK, row0=0, seq_len=seq)
            mix, ss = _retention(qkvg, gn_g, gn_b, state_c_S[j], chunk=ts, row0=mp, seq_len=ts, prev_out=mix)
            c_sp.append(sp); c_ss.append(ss)
            w_out = c_w_out[j]
        else:
            heads = FOX_HEADS
            hd = d // heads
            qkv, logf, k_p, k_s, v_p, v_s = _qkv_proj(
                x, d_w_in[j, :, :3 * d].astype(BF16), _pad_cols(d_w_in[j, :, 3 * d:], 128).astype(BF16),
                _row_vec(d_b_f[j], 128), n_scaled=d // 1024, scale=hd ** -0.5 * LOG2E, n_raw=0, kv_rows=(mp, ms))
            tri = jnp.stack([_cumsum_matrix(ROW_TILE), _cumsum_matrix(ts)])
            c = _running_sum(logf, tri, seq // ROW_TILE, n_prompt_tiles)
            mix = _fox_prompt(qkv, c, heads=heads, n_seq=nbp, seq_len=seq, tq=ATTN_BLOCK, hps=ATTN_HEADS_PER_STEP,
                              sub=ATTN_KEY_SUB)
            mix = _fox_sample(qkv, c, cache_d_k[j], cache_d_v[j],
                              jnp.transpose(cache_d_logf[j], (0, 2, 1)), mix, heads=heads, row0=mp, ts=ts)
            d_kp.append(k_p.reshape(nbp, seq, heads, hd)); d_ks.append(k_s.reshape(nbs, ts, heads, hd))
            d_vp.append(v_p.reshape(nbp, seq, heads, hd)); d_vs.append(v_s.reshape(nbs, ts, heads, hd))
            d_fp.append(logf[:mp, :heads].reshape(nbp, seq, heads))
            d_fs.append(logf[mp:, :heads].reshape(nbs, ts, heads))
            w_out = d_w_out[j]
        x = _out_ln(mix, w_out.astype(BF16), x, _row_vec(ln1_g[i]), _row_vec(ln1_b[i]), alpha)
        x = _ffn(x, ffn_w_in_b, ffn_w_out_b, i, _row_vec(ln2_g[i]), _row_vec(ln2_b[i]), alpha,
                 final_rows=(mp, ms) if i == depth - 1 else None)

    y_prompt, y_sample = x
    return (y_prompt.reshape(nbp, seq, d), y_sample.reshape(nbs, ts, d), jnp.stack(a_vs),
            jnp.stack(b_cp), jnp.stack(b_np), jnp.stack(b_mp),
            jnp.stack(b_cs), jnp.stack(b_ns), jnp.stack(b_ms),
            jnp.stack(c_sp), jnp.stack(c_ss),
            jnp.stack(d_kp), jnp.stack(d_vp), jnp.stack(d_fp),
            jnp.stack(d_ks), jnp.stack(d_vs), jnp.stack(d_fs))
```

```python
import functools
import math

import jax
import jax.numpy as jnp
from jax import lax
from jax.experimental import pallas as pl
from jax.experimental.pallas import tpu as pltpu

F32 = jnp.float32
BF16 = jnp.bfloat16

LN_EPS = 1e-5
ROPE_BASE = 10000.0
GMLP_BLOCK = 128
GMLP_CHUNK = 64
GMLP_GROUPS = 8
MLSTM_HEADS = 8
RET_HEADS = 8
FOX_HEADS = 16
ROW_TILE = 256
SEQ_CHUNK = 256
ATTN_BLOCK = 512
ATTN_HEADS_PER_STEP = 4
PROJ_WIDE_COLS = 2048
ATTN_KEY_SUB = 256
LOG2E = math.log2(math.e)
VMEM_LIMIT = 56 * 2 ** 20

NT = (((1,), (1,)), ((), ()))
TN = (((0,), (0,)), ((), ()))


def _params(*sem):
    return pltpu.CompilerParams(dimension_semantics=sem, vmem_limit_bytes=VMEM_LIMIT)


def _row_tile(m, candidates):
    for t in candidates:
        if m % t == 0:
            return t
    raise ValueError(f"no row tile for {m}")


def _dot(a, b):
    return jnp.dot(a, b, preferred_element_type=F32)


def _dg(a, b, dims):
    return lax.dot_general(a, b, dims, preferred_element_type=F32)


def _split3(x):
    hi = x.astype(BF16)
    r1 = x - hi.astype(F32)
    mid = r1.astype(BF16)
    lo = (r1 - mid.astype(F32)).astype(BF16)
    return hi, mid, lo


def _dot_exact_lhs(mat_bf16, x):
    hi, mid, lo = _split3(x)
    return _dot(mat_bf16, hi) + _dot(mat_bf16, mid) + _dot(mat_bf16, lo)


def _transpose_exact(x):
    n = x.shape[1]
    eye = (lax.broadcasted_iota(jnp.int32, (n, n), 0) == lax.broadcasted_iota(jnp.int32, (n, n), 1)).astype(BF16)
    hi, mid, lo = _split3(x)
    return _dg(eye, hi, NT) + _dg(eye, mid, NT) + _dg(eye, lo, NT)


def _sigmoid(x):
    return 1.0 / (1.0 + jnp.exp(-x))


def _log_sigmoid(x):
    return jnp.minimum(x, 0.0) - jnp.log1p(jnp.exp(-jnp.abs(x)))


def _gelu_tanh(x):
    return 0.5 * x * (1.0 + jnp.tanh(math.sqrt(2.0 / math.pi) * (x + 0.044715 * (x * x * x))))


def _layer_norm(y, g, b):
    mu = jnp.mean(y, axis=-1, keepdims=True)
    d = y - mu
    var = jnp.mean(d * d, axis=-1, keepdims=True)
    return d * lax.rsqrt(var + LN_EPS) * g + b


def _matmul_row_parts(lhs_ref, w_ref, epilogue, cast=False):
    tm = lhs_ref.shape[0]
    n_parts = 3 if tm % 48 == 0 else 2
    part = tm // n_parts
    rows = [slice(p * part, (p + 1) * part) for p in range(n_parts)]

    def matmul(r):
        lhs = lhs_ref[r, :]
        return _dot(lhs.astype(BF16) if cast else lhs, w_ref[...])

    acc = matmul(rows[0])
    for p, r in enumerate(rows):
        nxt = matmul(rows[p + 1]) if p + 1 < n_parts else None
        epilogue(r, acc)
        acc = nxt


def _cast_rows_once(x_ref, xb_scr, j):
    @pl.when(j == 0)
    def _():
        xb_scr[...] = x_ref[...].astype(BF16)


def _gelu_proj_kernel(x_ref, w_ref, b_ref, o_ref):
    def epilogue(r, acc):
        o_ref[r, :] = _gelu_tanh(acc + b_ref[...]).astype(o_ref.dtype)

    _matmul_row_parts(x_ref, w_ref, epilogue, cast=True)


def _gelu_ln_proj_kernel(x_ref, w_ref, b_ref, g_ref, be_ref, o_ref, last_ref):
    def epilogue(r, acc):
        v = _layer_norm(_gelu_tanh(acc + b_ref[...]), g_ref[...], be_ref[...])
        o_ref[r, :] = v.astype(o_ref.dtype)
        last_ref[r, :] = v

    _matmul_row_parts(x_ref, w_ref, epilogue, cast=True)


def _gmlp_in(x, w_u, b_u, w_v, b_v, vn_g, vn_b):
    m, d = x.shape
    tm = _row_tile(m, (384, 256))
    n = w_u.shape[1]
    row = pl.BlockSpec((tm, d), lambda i: (i, 0))
    wsp = pl.BlockSpec((d, n), lambda i: (0, 0))
    vec = pl.BlockSpec((1, n), lambda i: (0, 0))
    out = pl.BlockSpec((tm, n), lambda i: (i, 0))
    u = pl.pallas_call(
        _gelu_proj_kernel, grid=(m // tm,), in_specs=[row, wsp, vec], out_specs=out,
        out_shape=jax.ShapeDtypeStruct((m, n), BF16), compiler_params=_params("arbitrary"),
        name="gmlp_u")(x, w_u, b_u)
    v, v_last = pl.pallas_call(
        _gelu_ln_proj_kernel, grid=(m // tm,), in_specs=[row, wsp, vec, vec, vec],
        out_specs=[out, pl.BlockSpec((tm, n), lambda i: (0, 0))],
        out_shape=[jax.ShapeDtypeStruct((m, n), BF16), jax.ShapeDtypeStruct((tm, n), F32)],
        compiler_params=_params("arbitrary"), name="gmlp_v")(x, w_v, b_v, vn_g, vn_b)
    return u, v, v_last


def _qkv_proj_kernel(x_ref, w_ref, wg_ref, bg_ref, o_ref, gate_ref, *refs, scaled_cols, scale, nb, tail, n_raw):
    kv_refs, xb_scr = refs[:-1], refs[-1]
    j = pl.program_id(1)
    _cast_rows_once(x_ref, xb_scr, j)

    @pl.when(j == 0)
    def _():
        g = _dot(xb_scr[...], wg_ref[...]) + bg_ref[...]
        lane = lax.broadcasted_iota(jnp.int32, g.shape, 1)
        gate_ref[...] = jnp.where(lane < n_raw, g, _log_sigmoid(g))

    tn = o_ref.shape[1]
    col = j * tn + lax.broadcasted_iota(jnp.int32, (1, tn), 1)
    col_scale = jnp.where(col < scaled_cols, scale, 1.0)
    accs = []

    def epilogue(r, acc):
        o_ref[r, :] = (acc * col_scale).astype(o_ref.dtype)
        accs.append(acc)

    _matmul_row_parts(xb_scr, w_ref, epilogue)

    if kv_refs:
        kp_ref, ks_ref, vp_ref, vs_ref = kv_refs
        tm, tn = o_ref.shape
        acc = jnp.concatenate(accs, axis=0)

        @pl.when(jnp.logical_and(j >= nb, j < 2 * nb))
        def _():
            kp_ref[...] = acc
            ks_ref[:, pl.ds(pl.multiple_of((j - nb) * tn, tn), tn)] = acc[tm - tail:]

        @pl.when(j >= 2 * nb)
        def _():
            vp_ref[...] = acc
            vs_ref[:, pl.ds(pl.multiple_of((j - 2 * nb) * tn, tn), tn)] = acc[tm - tail:]


def _qkv_proj(x, w, w_gate, b_gate, *, scaled_cols, scale, n_raw, kv_rows=None, tn=1024):
    m, d = x.shape
    n = w.shape[1]
    tm = _row_tile(m, (768, 512, 256))
    in_specs = [pl.BlockSpec((tm, d), lambda i, j: (i, 0)), pl.BlockSpec((d, tn), lambda i, j: (0, j)),
                pl.BlockSpec((d, 128), lambda i, j: (0, 0)), pl.BlockSpec((1, 128), lambda i, j: (0, 0))]
    out_specs = [pl.BlockSpec((tm, tn), lambda i, j: (i, j)), pl.BlockSpec((tm, 128), lambda i, j: (i, 0))]
    out_shape = [jax.ShapeDtypeStruct((m, n), BF16), jax.ShapeDtypeStruct((m, 128), F32)]
    nb, tail = n // (3 * tn), 0
    if kv_rows is not None:
        mp, tail = kv_rows
        assert mp + tail == m and tail <= tm
        for first in (nb, 2 * nb):
            col = functools.partial(lambda j, first: jnp.clip(j - first, 0, nb - 1), first=first)
            out_specs += [pl.BlockSpec((tm, tn), lambda i, j, col=col: (i, col(j))),
                          pl.BlockSpec((tail, n // 3), lambda i, j: (0, 0))]
            out_shape += [jax.ShapeDtypeStruct((mp, n // 3), F32), jax.ShapeDtypeStruct((tail, n // 3), F32)]
    return pl.pallas_call(
        functools.partial(_qkv_proj_kernel, scaled_cols=scaled_cols, scale=scale, nb=nb, tail=tail, n_raw=n_raw),
        grid=(m // tm, n // tn), in_specs=in_specs, out_specs=out_specs, out_shape=out_shape,
        scratch_shapes=[pltpu.VMEM((tm, d), BF16)],
        compiler_params=_params("arbitrary", "arbitrary"), name="qkv_proj")(x, w, w_gate, b_gate)


def _rot_proj_kernel(x_ref, w_ref, cos_ref, sin_ref, o_ref, xb_scr, *, n_q, n_k, k_scale, head):
    j = pl.program_id(1)
    _cast_rows_once(x_ref, xb_scr, j)
    half = head // 2
    rotated = j < n_q + n_k
    col_scale = jnp.where(jnp.logical_and(j >= n_q, rotated), k_scale, 1.0).astype(F32)

    def epilogue(r, acc):
        cos = jnp.where(rotated, cos_ref[r, :], 1.0)
        sin = jnp.where(rotated, sin_ref[r, :], 0.0)
        for h in range(acc.shape[1] // head):
            x1 = acc[:, h * head:h * head + half]
            x2 = acc[:, h * head + half:(h + 1) * head]
            o_ref[r, h * head:h * head + half] = ((x1 * cos - x2 * sin) * col_scale).astype(o_ref.dtype)
            o_ref[r, h * head + half:(h + 1) * head] = ((x1 * sin + x2 * cos) * col_scale).astype(o_ref.dtype)

    _matmul_row_parts(xb_scr, w_ref, epilogue)


def _rot_proj(x, w, cos, sin, *, n_q, n_k, k_scale, head, tn=1024):
    m, d = x.shape
    n = w.shape[1]
    tm = _row_tile(m, (768, 512, 256))
    half = head // 2
    return pl.pallas_call(
        functools.partial(_rot_proj_kernel, n_q=n_q, n_k=n_k, k_scale=k_scale, head=head),
        grid=(m // tm, n // tn),
        in_specs=[pl.BlockSpec((tm, d), lambda i, j: (i, 0)), pl.BlockSpec((d, tn), lambda i, j: (0, j)),
                  pl.BlockSpec((tm, half), lambda i, j: (i, 0)), pl.BlockSpec((tm, half), lambda i, j: (i, 0))],
        out_specs=pl.BlockSpec((tm, tn), lambda i, j: (i, j)),
        out_shape=jax.ShapeDtypeStruct((m, n), BF16), scratch_shapes=[pltpu.VMEM((tm, d), BF16)],
        compiler_params=_params("arbitrary", "arbitrary"), name="rot_proj")(x, w, cos, sin)


def _running_sum_kernel(lf_ref, tri_ref, c_ref, carry_ref, *, tiles_per_seq):
    i = pl.program_id(0)

    @pl.when(i % tiles_per_seq == 0)
    def _():
        carry_ref[...] = jnp.zeros_like(carry_ref)

    c = _dot_exact_lhs(tri_ref[...], lf_ref[...]) + carry_ref[...]
    c_ref[...] = c
    carry_ref[...] = c[-1:, :]


def _running_sum(lf, tri, tiles_per_seq, n_prompt_tiles):
    m = lf.shape[0]
    tm = ROW_TILE
    blk = pl.BlockSpec((tm, 128), lambda i: (i, 0))
    return pl.pallas_call(
        functools.partial(_running_sum_kernel, tiles_per_seq=tiles_per_seq), grid=(m // tm,),
        in_specs=[blk, pl.BlockSpec((None, tm, tm), lambda i: (i // n_prompt_tiles, 0, 0))],
        out_specs=blk, out_shape=jax.ShapeDtypeStruct((m, 128), F32),
        scratch_shapes=[pltpu.VMEM((1, 128), F32)],
        compiler_params=_params("arbitrary"), name="running_sum")(lf, tri)


def _out_ln_kernel(h_ref, w_ref, x_ref, g_ref, b_ref, o_ref, *, alpha):
    def epilogue(r, acc):
        o_ref[r, :] = _layer_norm(alpha * x_ref[r, :] + acc, g_ref[...], b_ref[...])

    _matmul_row_parts(h_ref, w_ref, epilogue)


def _out_ln(h, w, x, g, b, alpha):
    m, k = h.shape
    d = w.shape[1]
    tm = _row_tile(m, (768, 384, 256) if k <= d else (384, 256))
    vec = pl.BlockSpec((1, d), lambda i: (0, 0))
    row = pl.BlockSpec((tm, d), lambda i: (i, 0))
    return pl.pallas_call(
        functools.partial(_out_ln_kernel, alpha=alpha), grid=(m // tm,),
        in_specs=[pl.BlockSpec((tm, k), lambda i: (i, 0)),
                  pl.BlockSpec((k, d), lambda i: (0, 0), pipeline_mode=pl.Buffered(1)),
                  row, vec, vec],
        out_specs=row, out_shape=jax.ShapeDtypeStruct((m, d), F32),
        compiler_params=_params("arbitrary"), name="out_ln")(h, w, x, g, b)


def _ffn_kernel(x_ref, wg_ref, wu_ref, wo_ref, g_ref, b_ref, o_ref, *rest, alpha):
    tail_refs, xb_scr = rest[:-1], rest[-1]
    j = pl.program_id(1)
    _cast_rows_once(x_ref, xb_scr, j)

    @pl.when(j == 0)
    def _():
        o_ref[...] = jnp.zeros_like(o_ref)

    xb = xb_scr[...]
    gate = _dot(xb, wg_ref[...])
    up = _dot(xb, wu_ref[...])
    hid = (gate * _sigmoid(gate) * up).astype(BF16)
    o_ref[...] += _dot(hid, wo_ref[...])

    @pl.when(j == pl.num_programs(1) - 1)
    def _():
        y = _layer_norm(alpha * x_ref[...] + o_ref[...], g_ref[...], b_ref[...])
        o_ref[...] = y
        for t_ref in tail_refs:
            t_ref[...] = y[y.shape[0] - t_ref.shape[0]:]


def _ffn(x, w_in, w_out, layer, g, b, alpha, final_rows=None, th=512):
    m, d = x.shape
    hidden = w_out.shape[1]
    tm = _row_tile(m, (768, 512, 256))
    nh = hidden // th
    vec = pl.BlockSpec((1, d), lambda i, j: (0, 0))
    row = pl.BlockSpec((tm, d), lambda i, j: (i, 0))
    if final_rows is None:
        out_specs, out_shape = row, jax.ShapeDtypeStruct((m, d), F32)
    else:
        mp, tail = final_rows
        assert mp + tail == m and tail <= tm
        out_specs = [row, pl.BlockSpec((tail, d), lambda i, j: (0, 0))]
        out_shape = [jax.ShapeDtypeStruct((mp, d), F32), jax.ShapeDtypeStruct((tail, d), F32)]
    return pl.pallas_call(
        functools.partial(_ffn_kernel, alpha=alpha), grid=(m // tm, nh),
        in_specs=[row, pl.BlockSpec((None, d, th), lambda i, j: (layer, 0, j)),
                  pl.BlockSpec((None, d, th), lambda i, j: (layer, 0, j + nh)),
                  pl.BlockSpec((None, th, d), lambda i, j: (layer, j, 0)), vec, vec],
        out_specs=out_specs, out_shape=out_shape,
        scratch_shapes=[pltpu.VMEM((tm, d), BF16)],
        compiler_params=_params("arbitrary", "arbitrary"), name="ffn")(x, w_in, w_in, w_out, g, b)


def _gmlp_gate_kernel(u_ref, v_ref, w_ref, b_ref, y_ref, *, groups):
    gd = u_ref.shape[1] // groups
    for g in range(groups):
        sl = slice(g * gd, (g + 1) * gd)
        s = _dot(w_ref[g], v_ref[:, sl]) + b_ref[:, sl]
        y_ref[:, sl] = (u_ref[:, sl].astype(F32) * s).astype(y_ref.dtype)


def _gmlp_gate(u, v, w_sp, b_sp, n_prompt_tiles):
    m, d = u.shape
    tm = ROW_TILE
    groups = w_sp.shape[1]
    row = pl.BlockSpec((tm, d), lambda i: (i, 0))
    return pl.pallas_call(
        functools.partial(_gmlp_gate_kernel, groups=groups), grid=(m // tm,),
        in_specs=[row, row,
                  pl.BlockSpec((None, groups, tm, tm), lambda i: (i // n_prompt_tiles, 0, 0, 0)),
                  pl.BlockSpec((None, tm, d), lambda i: (i // n_prompt_tiles, 0, 0))],
        out_specs=row, out_shape=jax.ShapeDtypeStruct((m, d), BF16),
        compiler_params=_params("arbitrary"), name="gmlp_gate")(u, v, w_sp, b_sp)


def _mixer_out(prev_out, rows, cols):
    return jnp.zeros((rows, cols), BF16) if prev_out is None else prev_out


def _mlstm_kernel(qkv_ref, gt_ref, ng_ref, c0_ref, n0_ref, m0_ref, *refs, heads):
    h_ref, c_ref, n_ref, m_ref = refs[-4:]
    d = qkv_ref.shape[1] // 3
    q_ref, k_ref = qkv_ref.at[:, 0:d // 2], qkv_ref.at[:, d // 2:d]
    v_ref, og_ref = qkv_ref.at[:, d:2 * d], qkv_ref.at[:, 2 * d:3 * d]
    step = pl.program_id(1)
    L = q_ref.shape[0]
    dqk = q_ref.shape[1] // heads
    dv = v_ref.shape[1] // heads

    @pl.when(step == 0)
    def _():
        c_ref[...] = c0_ref[...]
        n_ref[...] = n0_ref[...]
        m_ref[...] = m0_ref[...]

    g = gt_ref[...]
    row = lax.broadcasted_iota(jnp.int32, (L, L), 0)
    col = lax.broadcasted_iota(jnp.int32, (L, L), 1)
    causal = col <= row
    bc = _dot_exact_lhs(causal.astype(BF16), g)
    g_t = _transpose_exact(g)
    bc_t = _transpose_exact(bc)

    for h in range(heads):
        a = bc[:, heads + h:heads + h + 1]
        li = g[:, h:h + 1]
        a_row = bc_t[heads + h:heads + h + 1, :]
        li_row = g_t[h:h + 1, :]
        m_prev = m_ref[h:h + 1, :]
        dmat = jnp.where(causal, a - a_row + li_row, -jnp.inf)
        inter = a + m_prev
        m_t = jnp.maximum(inter, jnp.max(dmat, axis=1, keepdims=True))
        w_intra = jnp.exp(dmat - m_t)
        w_inter = jnp.exp(inter - m_t)
        qh = q_ref[:, h * dqk:(h + 1) * dqk]
        kh = k_ref[:, h * dqk:(h + 1) * dqk]
        vh = v_ref[:, h * dv:(h + 1) * dv]
        c_old = c_ref[h]
        n_old = n_ref[h:h + 1, :]
        scores = _dg(qh, kh, NT) * w_intra
        num = _dot(scores.astype(BF16), vh) + w_inter * _dg(qh, c_old.astype(BF16), NT)
        den = (jnp.sum(scores, axis=1, keepdims=True)
               + w_inter * jnp.sum(qh.astype(F32) * n_old, axis=1, keepdims=True))
        hid = num / jnp.maximum(jnp.abs(den), jnp.exp(-m_t))
        sl = slice(h * dv, (h + 1) * dv)
        y = hid * lax.rsqrt(jnp.mean(hid * hid, axis=1, keepdims=True) + LN_EPS) * ng_ref[:, sl]
        h_ref[:, sl] = (y * _sigmoid(og_ref[:, sl].astype(F32))).astype(h_ref.dtype)

        m_new = m_t[L - 1:L, :]
        b_last = a[L - 1:L, :]
        w_c = jnp.exp(b_last + m_prev - m_new)
        w_s = jnp.exp(b_last - a + li - m_new)
        c_ref[h] = w_c * c_old + _dg((vh.astype(F32) * w_s).astype(BF16), kh, TN)
        n_ref[h:h + 1, :] = w_c * n_old + jnp.sum(kh.astype(F32) * w_s, axis=0, keepdims=True)
        m_ref[h:h + 1, :] = m_new


def _mlstm(qkv, gates, norm_g, c0, n0, m0, *, chunk, row0, seq_len, prev_out=None):
    m, width = qkv.shape
    d = width // 3
    nb, heads, dv, dqk = c0.shape
    nc = seq_len // chunk
    base = row0 // chunk

    def rb(b, c):
        return base + b * nc + c

    in_specs = [pl.BlockSpec((chunk, width), lambda b, c: (rb(b, c), 0)),
                pl.BlockSpec((chunk, 128), lambda b, c: (rb(b, c), 0)),
                pl.BlockSpec((1, d), lambda b, c: (0, 0)),
                pl.BlockSpec((None, heads, dv, dqk), lambda b, c: (b, 0, 0, 0)),
                pl.BlockSpec((None, heads, dqk), lambda b, c: (b, 0, 0)),
                pl.BlockSpec((None, heads, 1), lambda b, c: (b, 0, 0))]
    args = [qkv, gates, norm_g, c0, n0, m0.reshape(nb, heads, 1)]
    in_specs.append(pl.BlockSpec(memory_space=pl.ANY))
    args.append(_mixer_out(prev_out, m, d))
    aliases = {len(args) - 1: 0}
    outs = pl.pallas_call(
        functools.partial(_mlstm_kernel, heads=heads), grid=(nb, nc), in_specs=in_specs,
        out_specs=[pl.BlockSpec((chunk, d), lambda b, c: (rb(b, c), 0)),
                   pl.BlockSpec((None, heads, dv, dqk), lambda b, c: (b, 0, 0, 0)),
                   pl.BlockSpec((None, heads, dqk), lambda b, c: (b, 0, 0)),
                   pl.BlockSpec((None, heads, 1), lambda b, c: (b, 0, 0))],
        out_shape=[jax.ShapeDtypeStruct((m, d), BF16), jax.ShapeDtypeStruct(c0.shape, F32),
                   jax.ShapeDtypeStruct(n0.shape, F32), jax.ShapeDtypeStruct((nb, heads, 1), F32)],
        input_output_aliases=aliases,
        compiler_params=_params("arbitrary", "arbitrary"), name="mlstm")(*args)
    return outs[0], outs[1], outs[2], outs[3].reshape(nb, heads)


def _retention_kernel(qkvg_ref, gg_ref, gb_ref, s0_ref, *refs, heads):
    o_ref, s_ref = refs[-2:]
    d = qkvg_ref.shape[1] // 6
    q_ref, k_ref = qkvg_ref.at[:, 0:d], qkvg_ref.at[:, d:2 * d]
    v_ref, g_ref = qkvg_ref.at[:, 2 * d:4 * d], qkvg_ref.at[:, 4 * d:6 * d]
    step = pl.program_id(1)
    L = q_ref.shape[0]
    dk = q_ref.shape[1] // heads
    dv = v_ref.shape[1] // heads

    @pl.when(step == 0)
    def _():
        s_ref[...] = s0_ref[...]

    row = lax.broadcasted_iota(jnp.int32, (L, L), 0)
    col = lax.broadcasted_iota(jnp.int32, (L, L), 1)
    causal = col <= row
    lag = (row - col).astype(F32)
    t = lax.broadcasted_iota(jnp.int32, (L, 1), 0).astype(F32)

    for h in range(heads):
        log_gamma = math.log1p(-(2.0 ** (-5.0 - h)))
        decay = jnp.where(causal, jnp.exp(lag * log_gamma), 0.0)
        decay_q = jnp.exp((t + 1.0) * log_gamma)
        decay_k = jnp.exp((L - 1.0 - t) * log_gamma)
        decay_s = math.exp(L * log_gamma)
        qh = q_ref[:, h * dk:(h + 1) * dk]
        kh = k_ref[:, h * dk:(h + 1) * dk]
        vh = v_ref[:, h * dv:(h + 1) * dv]
        s_old = s_ref[h]
        scores = _dg(qh, kh, NT) * decay
        o = _dot(scores.astype(BF16), vh) + _dot(qh, s_old.astype(BF16)) * decay_q
        s_ref[h] = decay_s * s_old + _dg((kh.astype(F32) * decay_k).astype(BF16), vh, TN)
        sl = slice(h * dv, (h + 1) * dv)
        y = _layer_norm(o, gg_ref[:, sl], gb_ref[:, sl])
        gate = g_ref[:, sl].astype(F32)
        o_ref[:, sl] = (gate * _sigmoid(gate) * y).astype(o_ref.dtype)


def _retention(qkvg, gn_g, gn_b, s0, *, chunk, row0, seq_len, prev_out=None):
    m, width = qkvg.shape
    d = width // 6
    nb, heads, dk, dv = s0.shape
    nc = seq_len // chunk
    base = row0 // chunk

    def rb(b, c):
        return base + b * nc + c

    in_specs = [pl.BlockSpec((chunk, width), lambda b, c: (rb(b, c), 0)),
                pl.BlockSpec((1, 2 * d), lambda b, c: (0, 0)),
                pl.BlockSpec((1, 2 * d), lambda b, c: (0, 0)),
                pl.BlockSpec((None, heads, dk, dv), lambda b, c: (b, 0, 0, 0))]
    args = [qkvg, gn_g, gn_b, s0]
    in_specs.append(pl.BlockSpec(memory_space=pl.ANY))
    args.append(_mixer_out(prev_out, m, 2 * d))
    aliases = {len(args) - 1: 0}
    return pl.pallas_call(
        functools.partial(_retention_kernel, heads=heads), grid=(nb, nc), in_specs=in_specs,
        out_specs=[pl.BlockSpec((chunk, 2 * d), lambda b, c: (rb(b, c), 0)),
                   pl.BlockSpec((None, heads, dk, dv), lambda b, c: (b, 0, 0, 0))],
        out_shape=[jax.ShapeDtypeStruct((m, 2 * d), BF16), jax.ShapeDtypeStruct(s0.shape, F32)],
        input_output_aliases=aliases,
        compiler_params=_params("arbitrary", "arbitrary"), name="retention")(*args)


def _select_lane(block, lane_index):
    lane = lax.broadcasted_iota(jnp.int32, block.shape, 1)
    return jnp.sum(jnp.where(lane == lane_index, block, 0.0), axis=1, keepdims=True)


def _bias_tail(c_col, *, query):
    hi, mid, lo = (t.astype(F32) for t in _split3(c_col))
    lane = lax.broadcasted_iota(jnp.int32, (c_col.shape[0], 128), 1)
    if query:
        tail = jnp.where(lane == 0, hi, jnp.where(lane == 1, mid, jnp.where(lane == 2, lo,
                                                                             jnp.where(lane < 6, 1.0, 0.0))))
    else:
        tail = jnp.where(lane < 3, 1.0, jnp.where(lane == 3, -hi, jnp.where(lane == 4, -mid,
                                                                             jnp.where(lane == 5, -lo, 0.0))))
    return tail.astype(BF16)


def _fox_prompt_kernel(q_ref, k_ref, v_ref, c_ref, buf_ref, o_ref, ka_scr, vt_scr, m_scr, l_scr, acc_scr, *,
                       hps, sub):
    del buf_ref
    hg = pl.program_id(1)
    qi = pl.program_id(2)
    tq = tk = q_ref.shape[0]
    seq = k_ref.shape[0]
    hd = q_ref.shape[1] // hps

    @pl.when(qi == 0)
    def _():
        eye =(lax.broadcasted_iota(jnp.int32, (hd, hd), 0)
               == lax.broadcasted_iota(jnp.int32, (hd, hd), 1)).astype(BF16)

        def build(r, carry):
            r0 = pl.multiple_of(r * tk, tk)
            cblk = c_ref[pl.ds(r0, tk), :] * LOG2E
            for j in range(hps):
                ka_scr[j, pl.ds(r0, tk), 0:hd] = k_ref[pl.ds(r0, tk), j * hd:(j + 1) * hd]
                ka_scr[j, pl.ds(r0, tk), hd:2 * hd] = _bias_tail(_select_lane(cblk, hg * hps + j), query=False)
                vt_scr[j, :, pl.ds(r0, tk)] = _dg(eye, v_ref[pl.ds(r0, tk), j * hd:(j + 1) * hd], NT).astype(BF16)
            return carry
        lax.fori_loop(0, seq // tk, build, 0)

    cq_blk = c_ref[pl.ds(pl.multiple_of(qi * tq, tq), tq), :] * LOG2E
    qa = [jnp.concatenate([q_ref[:, j * hd:(j + 1) * hd],
                           _bias_tail(_select_lane(cq_blk, hg * hps + j), query=True)], axis=1)
          for j in range(hps)]
    m_scr[...] = jnp.full_like(m_scr, -jnp.inf)
    l_scr[...] = jnp.zeros_like(l_scr)
    acc_scr[...] = jnp.zeros_like(acc_scr)

    def logits(j, ks):
        return _dg(ka_scr[j, pl.ds(ks, sub), :], qa[j], NT)

    def absorb(j, st, ks, diag_offset):
        if diag_offset is not None:
            key = lax.broadcasted_iota(jnp.int32, st.shape, 0)
            qry = lax.broadcasted_iota(jnp.int32, st.shape, 1)
            st = jnp.where(key + diag_offset <= qry, st, -jnp.inf)
        m_old = m_scr[j]
        m_new = jnp.maximum(m_old, jnp.max(st, axis=0, keepdims=True))
        pt = jnp.exp2(st - m_new)
        alpha = jnp.exp2(m_old - m_new)
        l_scr[j] = alpha * l_scr[j] + jnp.sum(pt, axis=0, keepdims=True)
        acc_scr[j] = alpha * acc_scr[j] + _dot(vt_scr[j, :, pl.ds(ks, sub)], pt.astype(BF16))
        m_scr[j] = m_new

    def key_block(k0, st_cur, masked):
        units = [(j, d) for d in range(tk // sub) for j in range(hps)]
        start = lambda d: pl.multiple_of(k0 + d * sub, sub)
        for n, (j, d) in enumerate(units):
            if n + 1 < len(units):
                st_next = logits(units[n + 1][0], start(units[n + 1][1]))
            else:
                st_next = None if masked else logits(units[0][0], pl.multiple_of(k0 + tk, sub))
            absorb(j, st_cur, start(d), d * sub if masked else None)
            st_cur = st_next
        return st_cur

    def below_diagonal(kk, st):
        return key_block(pl.multiple_of(kk * tk, tk), st, False)

    st_diag = lax.fori_loop(0, qi, below_diagonal, logits(0, 0))
    key_block(pl.multiple_of(qi * tk, tk), st_diag, True)
    for j in range(hps):
        o_ref[:, j * hd:(j + 1) * hd] = jnp.transpose(acc_scr[j] / l_scr[j]).astype(o_ref.dtype)


def _fox_prompt(qkv, c, *, heads, n_seq, seq_len, tq, hps, sub):
    m, width = qkv.shape
    d = width // 3
    hd = d // heads
    nq = seq_len // tq
    ng = heads // hps
    return pl.pallas_call(
        functools.partial(_fox_prompt_kernel, hps=hps, sub=sub), grid=(n_seq, ng, nq),
        in_specs=[pl.BlockSpec((tq, hps * hd), lambda b, g, i: (b * nq + i, g)),
                  pl.BlockSpec((seq_len, hps * hd), lambda b, g, i: (b, ng + g), pipeline_mode=pl.Buffered(1)),
                  pl.BlockSpec((seq_len, hps * hd), lambda b, g, i: (b, 2 * ng + g), pipeline_mode=pl.Buffered(1)),
                  pl.BlockSpec((seq_len, 128), lambda b, g, i: (b, 0), pipeline_mode=pl.Buffered(1)),
                  pl.BlockSpec(memory_space=pl.ANY)],
        out_specs=pl.BlockSpec((tq, hps * hd), lambda b, g, i: (b * nq + i, g)),
        out_shape=jax.ShapeDtypeStruct((m, d), BF16), input_output_aliases={4: 0},
        scratch_shapes=[pltpu.VMEM((hps, seq_len, 2 * hd), BF16), pltpu.VMEM((hps, hd, seq_len), BF16),
                        pltpu.VMEM((hps, 1, tq), F32), pltpu.VMEM((hps, 1, tq), F32),
                        pltpu.VMEM((hps, hd, tq), F32)],
        compiler_params=_params("arbitrary", "arbitrary", "arbitrary"), name="fox_prompt")(
            qkv, qkv, qkv, c, _mixer_out(None, m, d))


def _fox_sample_kernel(q_ref, kn_ref, vn_ref, c_ref, kc_ref, vc_ref, lfc_ref, prev_ref, o_ref, d_scr, ct_scr):
    del prev_ref
    g = pl.program_id(1)
    past, hpg, hd = kc_ref.shape
    ts = q_ref.shape[0]

    @pl.when(g == 0)
    def _():
        later = (lax.broadcasted_iota(jnp.int32, (past, past), 0)
                 > lax.broadcasted_iota(jnp.int32, (past, past), 1)).astype(BF16)
        hi, mid, lo = _split3(lfc_ref[...])
        d_scr[...] = _dot(hi, later) + _dot(mid, later) + _dot(lo, later)
        ct_scr[...] = _transpose_exact(c_ref[...])

    c_blk = c_ref[...]
    row = lax.broadcasted_iota(jnp.int32, (ts, ts), 0)
    col = lax.broadcasted_iota(jnp.int32, (ts, ts), 1)
    for j in range(hpg):
        h = g * hpg + j
        cq = _select_lane(c_blk, h)
        d_row = d_scr[pl.ds(h, 1), :]
        cn_row = ct_scr[pl.ds(h, 1), :]
        sl = slice(j * hd, (j + 1) * hd)
        q = q_ref[:, sl]
        s_past = _dg(q, kc_ref[:, j, :].astype(BF16), NT) + (cq + d_row) * LOG2E
        s_new = _dg(q, kn_ref[:, sl], NT) + (cq - cn_row) * LOG2E
        s_new = jnp.where(col <= row, s_new, -jnp.inf)
        m = jnp.maximum(jnp.max(s_past, axis=1, keepdims=True), jnp.max(s_new, axis=1, keepdims=True))
        p_past = jnp.exp2(s_past - m)
        p_new = jnp.exp2(s_new - m)
        den = jnp.sum(p_past, axis=1, keepdims=True) + jnp.sum(p_new, axis=1, keepdims=True)
        o = _dot(p_past.astype(BF16), vc_ref[:, j, :].astype(BF16)) + _dot(p_new.astype(BF16), vn_ref[:, sl])
        o_ref[:, sl] = (o / den).astype(o_ref.dtype)


def _fox_sample(qkv, c, k_cache, v_cache, lf_cache_t, prev_out, *, heads, row0, ts, hpg=8):
    m, width = qkv.shape
    d = width // 3
    hd = d // heads
    n_streams, past = k_cache.shape[:2]
    base = row0 // ts
    ng = heads // hpg
    return pl.pallas_call(
        _fox_sample_kernel, grid=(n_streams, ng),
        in_specs=[pl.BlockSpec((ts, hpg * hd), lambda b, g: (base + b, g)),
                  pl.BlockSpec((ts, hpg * hd), lambda b, g: (base + b, ng + g)),
                  pl.BlockSpec((ts, hpg * hd), lambda b, g: (base + b, 2 * ng + g)),
                  pl.BlockSpec((ts, 128), lambda b, g: (base + b, 0)),
                  pl.BlockSpec((None, past, hpg, hd), lambda b, g: (b, 0, g, 0)),
                  pl.BlockSpec((None, past, hpg, hd), lambda b, g: (b, 0, g, 0)),
                  pl.BlockSpec((None, heads, past), lambda b, g: (b, 0, 0)),
                  pl.BlockSpec(memory_space=pl.ANY)],
        out_specs=pl.BlockSpec((ts, hpg * hd), lambda b, g: (base + b, g)),
        out_shape=jax.ShapeDtypeStruct((m, d), BF16),
        scratch_shapes=[pltpu.VMEM((heads, past), F32), pltpu.VMEM((128, ts), F32)],
        input_output_aliases={7: 0},
        compiler_params=_params("arbitrary", "arbitrary"), name="fox_sample")(
            qkv, qkv, qkv, c, k_cache, v_cache, lf_cache_t, prev_out)


def _pad_cols(w, n):
    return jnp.pad(w, ((0, 0), (0, n - w.shape[1])))


def _row_vec(v, n=None):
    v = v.astype(F32).reshape(1, -1)
    return v if n is None else _pad_cols(v, n)


def _block_diag(blocks, copies):
    g, p, _ = blocks.shape
    eye = jnp.eye(copies, dtype=blocks.dtype)
    return jnp.einsum("ab,gpq->gapbq", eye, blocks).reshape(g, copies * p, copies * p)


def _gmlp_spatial(w_s, b_s, block, group_dim):
    pos = jnp.arange(block)
    mask = (pos[None, :] // GMLP_CHUNK) <= (pos[:, None] // GMLP_CHUNK)
    ws = jnp.where(mask[None], w_s[:, :block, :block], 0.0)
    copies = ROW_TILE // block
    bias = jnp.repeat(jnp.tile(jnp.transpose(b_s[:, :block]), (copies, 1)), group_dim, axis=1)
    return _block_diag(ws, copies).astype(BF16), bias.astype(F32)


def _cumsum_matrix(block):
    r = jnp.arange(ROW_TILE)
    return ((r[None, :] <= r[:, None]) & (r[None, :] // block == r[:, None] // block)).astype(BF16)


def kernel(x_prompt, x_sample, state_b_C, state_b_n, state_b_m, state_c_S, cache_d_k, cache_d_v, cache_d_logf, a_w_in, a_b_in, a_vn_g, a_vn_b, a_w_s, a_b_s, a_w_out, b_w_in, b_b_gates, b_norm_g, b_w_out, c_w_in, c_gn_g, c_gn_b, c_w_out, d_w_in, d_b_f, d_w_out, ffn_w_in, ffn_w_out, ln1_g, ln1_b, ln2_g, ln2_b):
    nbp, seq, d = x_prompt.shape
    nbs, ts, _ = x_sample.shape
    mp, ms = nbp * seq, nbs * ts
    past = cache_d_k.shape[2]
    depth = ffn_w_in.shape[0]
    alpha = (2.0 * depth) ** 0.25
    assert ms == ROW_TILE and ROW_TILE % ts == 0 and seq % ATTN_BLOCK == 0 and seq % SEQ_CHUNK == 0
    n_prompt_tiles = mp // ROW_TILE

    x = jnp.concatenate([x_prompt.reshape(mp, d), x_sample.reshape(ms, d)], axis=0)
    ffn_w_in_b, ffn_w_out_b = ffn_w_in.astype(BF16), ffn_w_out.astype(BF16)

    a_vs = []
    b_cp, b_np, b_mp, b_cs, b_ns, b_ms = [], [], [], [], [], []
    c_sp, c_ss = [], []
    d_kp, d_vp, d_fp, d_ks, d_vs, d_fs = [], [], [], [], [], []

    for i in range(depth):
        kind, j = i % 4, i // 4
        if kind == 0:
            w_in = a_w_in[j].astype(BF16)
            u, v, v_last = _gmlp_in(x, w_in[:, :d], _row_vec(a_b_in[j, :d]), w_in[:, d:], _row_vec(a_b_in[j, d:]),
                                    _row_vec(a_vn_g[j]), _row_vec(a_vn_b[j]))
            gd = d // GMLP_GROUPS
            wp, bp = _gmlp_spatial(a_w_s[j], a_b_s[j], GMLP_BLOCK, gd)
            wsm, bsm = _gmlp_spatial(a_w_s[j], a_b_s[j], ts, gd)
            mix = _gmlp_gate(u, v, jnp.stack([wp, wsm]), jnp.stack([bp, bsm]), n_prompt_tiles)
            w_out = a_w_out[j]
            a_vs.append(v_last[v_last.shape[0] - ms:].reshape(nbs, ts, d))
        elif kind == 1:
            heads = MLSTM_HEADS
            n_main = b_w_in.shape[2] - 2 * heads
            dqk = (n_main - 2 * d) // (2 * heads)
            qkv, gates = _qkv_proj(x, b_w_in[j, :, :n_main].astype(BF16),
                                   _pad_cols(b_w_in[j, :, n_main:], 128).astype(BF16), _row_vec(b_b_gates[j], 128),
                                   scaled_cols=heads * dqk, scale=dqk ** -0.5, n_raw=heads, tn=PROJ_WIDE_COLS)
            norm_g = _row_vec(b_norm_g[j])
            zc = jnp.zeros((nbp,) + state_b_C.shape[2:], F32)
            zn = jnp.zeros((nbp,) + state_b_n.shape[2:], F32)
            zm = jnp.zeros((nbp,) + state_b_m.shape[2:], F32)
            mix, cp, np_, mp_state = _mlstm(qkv, gates, norm_g, zc, zn, zm, chunk=SEQ_CHUNK, row0=0, seq_len=seq)
            mix, cs, ns, ms_state = _mlstm(qkv, gates, norm_g, state_b_C[j], state_b_n[j], state_b_m[j],
                                           chunk=ts, row0=mp, seq_len=ts, prev_out=mix)
            b_cp.append(cp); b_np.append(np_); b_mp.append(mp_state)
            b_cs.append(cs); b_ns.append(ns); b_ms.append(ms_state)
            w_out = b_w_out[j]
        elif kind == 2:
            heads = RET_HEADS
            dk = d // heads
            half = dk // 2
            inv = ROPE_BASE ** (-jnp.arange(half, dtype=F32) / half)
            pos = jnp.concatenate([jnp.tile(jnp.arange(seq), nbp), past + jnp.tile(jnp.arange(ts), nbs)])
            ang = pos.astype(F32)[:, None] * inv[None, :]
            qkvg = _rot_proj(x, c_w_in[j].astype(BF16), jnp.cos(ang), jnp.sin(ang),
                             n_q=d // PROJ_WIDE_COLS, n_k=d // PROJ_WIDE_COLS, k_scale=dk ** -0.5, head=dk,
                             tn=PROJ_WIDE_COLS)
            gn_g, gn_b = _row_vec(c_gn_g[j]), _row_vec(c_gn_b[j])
            zs = jnp.zeros((nbp,) + state_c_S.shape[2:], F32)
            mix, sp = _retention(qkvg, gn_g, gn_b, zs, chunk=SEQ_CHUNK, row0=0, seq_len=seq)
            mix, ss = _retention(qkvg, gn_g, gn_b, state_c_S[j], chunk=ts, row0=mp, seq_len=ts, prev_out=mix)
            c_sp.append(sp); c_ss.append(ss)
            w_out = c_w_out[j]
        else:
            heads = FOX_HEADS
            hd = d // heads
            qkv, logf, k_p, k_s, v_p, v_s = _qkv_proj(
                x, d_w_in[j, :, :3 * d].astype(BF16), _pad_cols(d_w_in[j, :, 3 * d:], 128).astype(BF16),
                _row_vec(d_b_f[j], 128), scaled_cols=d, scale=hd ** -0.5 * LOG2E, n_raw=0, kv_rows=(mp, ms))
            tri = jnp.stack([_cumsum_matrix(ROW_TILE), _cumsum_matrix(ts)])
            c = _running_sum(logf, tri, seq // ROW_TILE, n_prompt_tiles)
            mix = _fox_prompt(qkv, c, heads=heads, n_seq=nbp, seq_len=seq, tq=ATTN_BLOCK, hps=ATTN_HEADS_PER_STEP,
                              sub=ATTN_KEY_SUB)
            mix = _fox_sample(qkv, c, cache_d_k[j], cache_d_v[j],
                              jnp.transpose(cache_d_logf[j], (0, 2, 1)), mix, heads=heads, row0=mp, ts=ts)
            d_kp.append(k_p.reshape(nbp, seq, heads, hd)); d_ks.append(k_s.reshape(nbs, ts, heads, hd))
            d_vp.append(v_p.reshape(nbp, seq, heads, hd)); d_vs.append(v_s.reshape(nbs, ts, heads, hd))
            d_fp.append(logf[:mp, :heads].reshape(nbp, seq, heads))
            d_fs.append(logf[mp:, :heads].reshape(nbs, ts, heads))
            w_out = d_w_out[j]
        x = _out_ln(mix, w_out.astype(BF16), x, _row_vec(ln1_g[i]), _row_vec(ln1_b[i]), alpha)
        x = _ffn(x, ffn_w_in_b, ffn_w_out_b, i, _row_vec(ln2_g[i]), _row_vec(ln2_b[i]), alpha,
                 final_rows=(mp, ms) if i == depth - 1 else None)

    y_prompt, y_sample = x
    return (y_prompt.reshape(nbp, seq, d), y_sample.reshape(nbs, ts, d), jnp.stack(a_vs),
            jnp.stack(b_cp), jnp.stack(b_np), jnp.stack(b_mp),
            jnp.stack(b_cs), jnp.stack(b_ns), jnp.stack(b_ms),
            jnp.stack(c_sp), jnp.stack(c_ss),
            jnp.stack(d_kp), jnp.stack(d_vp), jnp.stack(d_fp),
            jnp.stack(d_ks), jnp.stack(d_vs), jnp.stack(d_fs))
```

```python
import functools
import math

import jax
import jax.numpy as jnp
from jax import lax
from jax.experimental import pallas as pl
from jax.experimental.pallas import tpu as pltpu

F32 = jnp.float32
BF16 = jnp.bfloat16

LN_EPS = 1e-5
ROPE_BASE = 10000.0
GMLP_BLOCK = 128
GMLP_CHUNK = 64
GMLP_GROUPS = 8
MLSTM_HEADS = 8
RET_HEADS = 8
FOX_HEADS = 16
LANES = 128
ROW_TILE = 256
SEQ_CHUNK = 256
ATTN_BLOCK = 512
ATTN_HEADS_PER_STEP = 4
PROJ_WIDE_COLS = 2048
ATTN_KEY_SUB = 256
LOG2E = math.log2(math.e)
VMEM_LIMIT = 56 * 2 ** 20

NT = (((1,), (1,)), ((), ()))
TN = (((0,), (0,)), ((), ()))


def _params(*sem):
    return pltpu.CompilerParams(dimension_semantics=sem, vmem_limit_bytes=VMEM_LIMIT)


def _row_tile(m, candidates):
    for t in candidates:
        if m % t == 0:
            return t
    raise ValueError(f"no row tile for {m}")


def _dot(a, b):
    return jnp.dot(a, b, preferred_element_type=F32)


def _dg(a, b, dims):
    return lax.dot_general(a, b, dims, preferred_element_type=F32)


def _split3(x):
    hi = x.astype(BF16)
    r1 = x - hi.astype(F32)
    mid = r1.astype(BF16)
    lo = (r1 - mid.astype(F32)).astype(BF16)
    return hi, mid, lo


def _dot_exact_lhs(mat_bf16, x):
    hi, mid, lo = _split3(x)
    return _dot(mat_bf16, hi) + _dot(mat_bf16, mid) + _dot(mat_bf16, lo)


def _transpose_exact(x):
    n = x.shape[1]
    eye = (lax.broadcasted_iota(jnp.int32, (n, n), 0) == lax.broadcasted_iota(jnp.int32, (n, n), 1)).astype(BF16)
    hi, mid, lo = _split3(x)
    return _dg(eye, hi, NT) + _dg(eye, mid, NT) + _dg(eye, lo, NT)


def _sigmoid(x):
    return 1.0 / (1.0 + jnp.exp(-x))


def _log_sigmoid(x):
    return jnp.minimum(x, 0.0) - jnp.log1p(jnp.exp(-jnp.abs(x)))


def _gelu_tanh(x):
    return 0.5 * x * (1.0 + jnp.tanh(math.sqrt(2.0 / math.pi) * (x + 0.044715 * (x * x * x))))


def _layer_norm(y, g, b):
    mu = jnp.mean(y, axis=-1, keepdims=True)
    d = y - mu
    var = jnp.mean(d * d, axis=-1, keepdims=True)
    return d * lax.rsqrt(var + LN_EPS) * g + b


def _matmul_row_parts(lhs_ref, w_ref, epilogue, cast=False):
    tm = lhs_ref.shape[0]
    n_parts = 3 if tm % 48 == 0 else 2
    part = tm // n_parts
    rows = [slice(p * part, (p + 1) * part) for p in range(n_parts)]

    def matmul(r):
        lhs = lhs_ref[r, :]
        return _dot(lhs.astype(BF16) if cast else lhs, w_ref[...])

    acc = matmul(rows[0])
    for p, r in enumerate(rows):
        nxt = matmul(rows[p + 1]) if p + 1 < n_parts else None
        epilogue(r, acc)
        acc = nxt


def _cast_rows_once(x_ref, xb_scr, j):
    @pl.when(j == 0)
    def _():
        xb_scr[...] = x_ref[...].astype(BF16)


def _gelu_proj_kernel(x_ref, w_ref, b_ref, o_ref):
    def epilogue(r, acc):
        o_ref[r, :] = _gelu_tanh(acc + b_ref[...]).astype(o_ref.dtype)

    _matmul_row_parts(x_ref, w_ref, epilogue, cast=True)


def _gelu_ln_proj_kernel(x_ref, w_ref, b_ref, g_ref, be_ref, o_ref, last_ref):
    def epilogue(r, acc):
        v = _layer_norm(_gelu_tanh(acc + b_ref[...]), g_ref[...], be_ref[...])
        o_ref[r, :] = v.astype(o_ref.dtype)
        last_ref[r, :] = v

    _matmul_row_parts(x_ref, w_ref, epilogue, cast=True)


def _gmlp_in(x, w_u, b_u, w_v, b_v, vn_g, vn_b):
    m, d = x.shape
    tm = _row_tile(m, (384, 256))
    n = w_u.shape[1]
    row = pl.BlockSpec((tm, d), lambda i: (i, 0))
    wsp = pl.BlockSpec((d, n), lambda i: (0, 0))
    vec = pl.BlockSpec((1, n), lambda i: (0, 0))
    out = pl.BlockSpec((tm, n), lambda i: (i, 0))
    u = pl.pallas_call(
        _gelu_proj_kernel, grid=(m // tm,), in_specs=[row, wsp, vec], out_specs=out,
        out_shape=jax.ShapeDtypeStruct((m, n), BF16), compiler_params=_params("arbitrary"),
        name="gmlp_u")(x, w_u, b_u)
    v, v_last = pl.pallas_call(
        _gelu_ln_proj_kernel, grid=(m // tm,), in_specs=[row, wsp, vec, vec, vec],
        out_specs=[out, pl.BlockSpec((tm, n), lambda i: (0, 0))],
        out_shape=[jax.ShapeDtypeStruct((m, n), BF16), jax.ShapeDtypeStruct((tm, n), F32)],
        compiler_params=_params("arbitrary"), name="gmlp_v")(x, w_v, b_v, vn_g, vn_b)
    return u, v, v_last


def _qkv_proj_kernel(x_ref, w_ref, wg_ref, bg_ref, o_ref, gate_ref, *refs, scaled_cols, scale, nb, tail, n_raw):
    kv_refs, xb_scr = refs[:-1], refs[-1]
    j = pl.program_id(1)
    _cast_rows_once(x_ref, xb_scr, j)

    @pl.when(j == 0)
    def _():
        g = _dot(xb_scr[...], wg_ref[...]) + bg_ref[...]
        lane = lax.broadcasted_iota(jnp.int32, g.shape, 1)
        gate_ref[...] = jnp.where(lane < n_raw, g, _log_sigmoid(g))

    tn = o_ref.shape[1]
    col = j * tn + lax.broadcasted_iota(jnp.int32, (1, tn), 1)
    col_scale = jnp.where(col < scaled_cols, scale, 1.0)
    accs = []

    def epilogue(r, acc):
        o_ref[r, :] = (acc * col_scale).astype(o_ref.dtype)
        accs.append(acc)

    _matmul_row_parts(xb_scr, w_ref, epilogue)

    if kv_refs:
        kp_ref, ks_ref, vp_ref, vs_ref = kv_refs
        tm, tn = o_ref.shape
        acc = jnp.concatenate(accs, axis=0)

        @pl.when(jnp.logical_and(j >= nb, j < 2 * nb))
        def _():
            kp_ref[...] = acc
            ks_ref[:, pl.ds(pl.multiple_of((j - nb) * tn, tn), tn)] = acc[tm - tail:]

        @pl.when(j >= 2 * nb)
        def _():
            vp_ref[...] = acc
            vs_ref[:, pl.ds(pl.multiple_of((j - 2 * nb) * tn, tn), tn)] = acc[tm - tail:]


def _qkv_proj(x, w, w_gate, b_gate, *, scaled_cols, scale, n_raw, kv_rows=None, tn=1024):
    m, d = x.shape
    n = w.shape[1]
    tm = _row_tile(m, (768, 512, 256))
    in_specs = [pl.BlockSpec((tm, d), lambda i, j: (i, 0)), pl.BlockSpec((d, tn), lambda i, j: (0, j)),
                pl.BlockSpec((d, 128), lambda i, j: (0, 0)), pl.BlockSpec((1, 128), lambda i, j: (0, 0))]
    out_specs = [pl.BlockSpec((tm, tn), lambda i, j: (i, j)), pl.BlockSpec((tm, 128), lambda i, j: (i, 0))]
    out_shape = [jax.ShapeDtypeStruct((m, n), BF16), jax.ShapeDtypeStruct((m, 128), F32)]
    nb, tail = n // (3 * tn), 0
    if kv_rows is not None:
        mp, tail = kv_rows
        assert mp + tail == m and tail <= tm
        for first in (nb, 2 * nb):
            col = functools.partial(lambda j, first: jnp.clip(j - first, 0, nb - 1), first=first)
            out_specs += [pl.BlockSpec((tm, tn), lambda i, j, col=col: (i, col(j))),
                          pl.BlockSpec((tail, n // 3), lambda i, j: (0, 0))]
            out_shape += [jax.ShapeDtypeStruct((mp, n // 3), F32), jax.ShapeDtypeStruct((tail, n // 3), F32)]
    return pl.pallas_call(
        functools.partial(_qkv_proj_kernel, scaled_cols=scaled_cols, scale=scale, nb=nb, tail=tail, n_raw=n_raw),
        grid=(m // tm, n // tn), in_specs=in_specs, out_specs=out_specs, out_shape=out_shape,
        scratch_shapes=[pltpu.VMEM((tm, d), BF16)],
        compiler_params=_params("arbitrary", "arbitrary"), name="qkv_proj")(x, w, w_gate, b_gate)


def _rot_proj_kernel(x_ref, w_ref, cos_ref, sin_ref, o_ref, xb_scr, *, n_q, n_k, k_scale, head):
    j = pl.program_id(1)
    _cast_rows_once(x_ref, xb_scr, j)
    half = head // 2
    rotated = j < n_q + n_k
    col_scale = jnp.where(jnp.logical_and(j >= n_q, rotated), k_scale, 1.0).astype(F32)

    def epilogue(r, acc):
        cos = jnp.where(rotated, cos_ref[r, :], 1.0)
        sin = jnp.where(rotated, sin_ref[r, :], 0.0)
        for h in range(acc.shape[1] // head):
            x1 = acc[:, h * head:h * head + half]
            x2 = acc[:, h * head + half:(h + 1) * head]
            o_ref[r, h * head:h * head + half] = ((x1 * cos - x2 * sin) * col_scale).astype(o_ref.dtype)
            o_ref[r, h * head + half:(h + 1) * head] = ((x1 * sin + x2 * cos) * col_scale).astype(o_ref.dtype)

    _matmul_row_parts(xb_scr, w_ref, epilogue)


def _rot_proj(x, w, cos, sin, *, n_q, n_k, k_scale, head, tn=1024):
    m, d = x.shape
    n = w.shape[1]
    tm = _row_tile(m, (768, 512, 256))
    half = head // 2
    return pl.pallas_call(
        functools.partial(_rot_proj_kernel, n_q=n_q, n_k=n_k, k_scale=k_scale, head=head),
        grid=(m // tm, n // tn),
        in_specs=[pl.BlockSpec((tm, d), lambda i, j: (i, 0)), pl.BlockSpec((d, tn), lambda i, j: (0, j)),
                  pl.BlockSpec((tm, half), lambda i, j: (i, 0)), pl.BlockSpec((tm, half), lambda i, j: (i, 0))],
        out_specs=pl.BlockSpec((tm, tn), lambda i, j: (i, j)),
        out_shape=jax.ShapeDtypeStruct((m, n), BF16), scratch_shapes=[pltpu.VMEM((tm, d), BF16)],
        compiler_params=_params("arbitrary", "arbitrary"), name="rot_proj")(x, w, cos, sin)


def _running_sum_kernel(lf_ref, tri_ref, c_ref, carry_ref, *, tiles_per_seq):
    i = pl.program_id(0)

    @pl.when(i % tiles_per_seq == 0)
    def _():
        carry_ref[...] = jnp.zeros_like(carry_ref)

    c = _dot_exact_lhs(tri_ref[...], lf_ref[...]) + carry_ref[...]
    c_ref[...] = c
    carry_ref[...] = c[-1:, :]


def _running_sum(lf, tri, tiles_per_seq, n_prompt_tiles):
    m = lf.shape[0]
    tm = ROW_TILE
    blk = pl.BlockSpec((tm, 128), lambda i: (i, 0))
    return pl.pallas_call(
        functools.partial(_running_sum_kernel, tiles_per_seq=tiles_per_seq), grid=(m // tm,),
        in_specs=[blk, pl.BlockSpec((None, tm, tm), lambda i: (i // n_prompt_tiles, 0, 0))],
        out_specs=blk, out_shape=jax.ShapeDtypeStruct((m, 128), F32),
        scratch_shapes=[pltpu.VMEM((1, 128), F32)],
        compiler_params=_params("arbitrary"), name="running_sum")(lf, tri)


def _out_ln_kernel(h_ref, w_ref, x_ref, g_ref, b_ref, o_ref, *, alpha):
    def epilogue(r, acc):
        o_ref[r, :] = _layer_norm(alpha * x_ref[r, :] + acc, g_ref[...], b_ref[...])

    _matmul_row_parts(h_ref, w_ref, epilogue)


def _out_ln(h, w, x, g, b, alpha):
    m, k = h.shape
    d = w.shape[1]
    tm = _row_tile(m, (768, 384, 256) if k <= d else (384, 256))
    vec = pl.BlockSpec((1, d), lambda i: (0, 0))
    row = pl.BlockSpec((tm, d), lambda i: (i, 0))
    return pl.pallas_call(
        functools.partial(_out_ln_kernel, alpha=alpha), grid=(m // tm,),
        in_specs=[pl.BlockSpec((tm, k), lambda i: (i, 0)),
                  pl.BlockSpec((k, d), lambda i: (0, 0), pipeline_mode=pl.Buffered(1)),
                  row, vec, vec],
        out_specs=row, out_shape=jax.ShapeDtypeStruct((m, d), F32),
        compiler_params=_params("arbitrary"), name="out_ln")(h, w, x, g, b)


def _ffn_kernel(x_ref, wg_ref, wu_ref, wo_ref, g_ref, b_ref, o_ref, *rest, alpha):
    tail_refs, xb_scr = rest[:-1], rest[-1]
    j = pl.program_id(1)
    _cast_rows_once(x_ref, xb_scr, j)

    @pl.when(j == 0)
    def _():
        o_ref[...] = jnp.zeros_like(o_ref)

    xb = xb_scr[...]
    gate = _dot(xb, wg_ref[...])
    up = _dot(xb, wu_ref[...])
    hid = (gate * _sigmoid(gate) * up).astype(BF16)
    o_ref[...] += _dot(hid, wo_ref[...])

    @pl.when(j == pl.num_programs(1) - 1)
    def _():
        y = _layer_norm(alpha * x_ref[...] + o_ref[...], g_ref[...], b_ref[...])
        o_ref[...] = y
        for t_ref in tail_refs:
            t_ref[...] = y[y.shape[0] - t_ref.shape[0]:]


def _ffn(x, w_in, w_out, layer, g, b, alpha, final_rows=None, th=512):
    m, d = x.shape
    hidden = w_out.shape[1]
    tm = _row_tile(m, (768, 512, 256))
    nh = hidden // th
    vec = pl.BlockSpec((1, d), lambda i, j: (0, 0))
    row = pl.BlockSpec((tm, d), lambda i, j: (i, 0))
    if final_rows is None:
        out_specs, out_shape = row, jax.ShapeDtypeStruct((m, d), F32)
    else:
        mp, tail = final_rows
        assert mp + tail == m and tail <= tm
        out_specs = [row, pl.BlockSpec((tail, d), lambda i, j: (0, 0))]
        out_shape = [jax.ShapeDtypeStruct((mp, d), F32), jax.ShapeDtypeStruct((tail, d), F32)]
    return pl.pallas_call(
        functools.partial(_ffn_kernel, alpha=alpha), grid=(m // tm, nh),
        in_specs=[row, pl.BlockSpec((None, d, th), lambda i, j: (layer, 0, j)),
                  pl.BlockSpec((None, d, th), lambda i, j: (layer, 0, j + nh)),
                  pl.BlockSpec((None, th, d), lambda i, j: (layer, j, 0)), vec, vec],
        out_specs=out_specs, out_shape=out_shape,
        scratch_shapes=[pltpu.VMEM((tm, d), BF16)],
        compiler_params=_params("arbitrary", "arbitrary"), name="ffn")(x, w_in, w_in, w_out, g, b)


def _gmlp_gate_kernel(u_ref, v_ref, w_ref, b_ref, y_ref, *, groups):
    gd = u_ref.shape[1] // groups
    for g in range(groups):
        sl = slice(g * gd, (g + 1) * gd)
        s = _dot(w_ref[g], v_ref[:, sl]) + b_ref[:, sl]
        y_ref[:, sl] = (u_ref[:, sl].astype(F32) * s).astype(y_ref.dtype)


def _gmlp_gate(u, v, w_sp, b_sp, n_prompt_tiles):
    m, d = u.shape
    tm = ROW_TILE
    groups = w_sp.shape[1]
    row = pl.BlockSpec((tm, d), lambda i: (i, 0))
    return pl.pallas_call(
        functools.partial(_gmlp_gate_kernel, groups=groups), grid=(m // tm,),
        in_specs=[row, row,
                  pl.BlockSpec((None, groups, tm, tm), lambda i: (i // n_prompt_tiles, 0, 0, 0)),
                  pl.BlockSpec((None, tm, d), lambda i: (i // n_prompt_tiles, 0, 0))],
        out_specs=row, out_shape=jax.ShapeDtypeStruct((m, d), BF16),
        compiler_params=_params("arbitrary"), name="gmlp_gate")(u, v, w_sp, b_sp)


def _chunked_mixer_kernel(step_fn, n_in, n_out, n_chunks, n_main, *refs):
    ins, outs = refs[:n_in], refs[len(refs) - n_out:]
    s = pl.program_id(0)
    if len(refs) > n_in + n_out:
        @pl.when(s == n_main)
        def _():
            outs[0][...] = refs[n_in][...]

    @pl.when(s < n_main)
    def _():
        step_fn(*ins, *outs, first=(s % n_chunks == 0))


def _mlstm_step(qkv_ref, gt_ref, ng_ref, c0_ref, n0_ref, m0_ref, h_ref, c_ref, n_ref, m_ref, *, first, heads):
    d = qkv_ref.shape[1] // 3
    q_ref, k_ref = qkv_ref.at[:, 0:d // 2], qkv_ref.at[:, d // 2:d]
    v_ref, og_ref = qkv_ref.at[:, d:2 * d], qkv_ref.at[:, 2 * d:3 * d]
    L = q_ref.shape[0]
    dqk = q_ref.shape[1] // heads
    dv = v_ref.shape[1] // heads

    @pl.when(first)
    def _():
        c_ref[...] = c0_ref[...]
        n_ref[...] = n0_ref[...]
        m_ref[...] = m0_ref[...]

    g = gt_ref[...]
    row = lax.broadcasted_iota(jnp.int32, (L, L), 0)
    col = lax.broadcasted_iota(jnp.int32, (L, L), 1)
    causal = col <= row
    bc = _dot_exact_lhs(causal.astype(BF16), g)
    g_t = _transpose_exact(g)
    bc_t = _transpose_exact(bc)

    ones_cols = jnp.ones((L, LANES), BF16)
    mean_cols = jnp.full((dv, LANES), 1.0 / dv, BF16)

    def rep(col):
        return jnp.broadcast_to(col, (L, LANES))

    def wide(x_rep, cols):
        return x_rep[:, :cols] if cols < LANES else jnp.concatenate([x_rep] * (cols // LANES), axis=1)

    for h in range(heads):
        a = rep(bc[:, heads + h:heads + h + 1])
        li = rep(g[:, h:h + 1])
        a_row = bc_t[heads + h:heads + h + 1, :]
        li_row = g_t[h:h + 1, :]
        m_prev = m_ref[h:h + 1, :]
        dmat = jnp.where(causal, wide(a, L) - a_row + li_row, -jnp.inf)
        inter = a + m_prev
        m_t = jnp.maximum(inter, rep(jnp.max(dmat, axis=1, keepdims=True)))
        w_intra = jnp.exp(dmat - wide(m_t, L))
        w_inter = jnp.exp(inter - m_t)
        qh = q_ref[:, h * dqk:(h + 1) * dqk]
        kh = k_ref[:, h * dqk:(h + 1) * dqk]
        vh = v_ref[:, h * dv:(h + 1) * dv]
        c_old = c_ref[h]
        n_old = n_ref[h:h + 1, :]
        scores = (_dg(qh, kh, NT) * w_intra).astype(BF16)
        v_ones = jnp.concatenate([vh, ones_cols], axis=1)
        c_n = jnp.concatenate([c_old, jnp.broadcast_to(n_old, (LANES, dqk))], axis=0).astype(BF16)
        both = _dot(scores, v_ones) + wide(w_inter, dv + LANES) * _dg(qh, c_n, NT)
        inv = 1.0 / jnp.maximum(jnp.abs(both[:, dv:]), jnp.exp(-m_t))
        hid = both[:, :dv] * wide(inv, dv)
        mean_sq = _dot((hid * hid).astype(BF16), mean_cols)
        sl = slice(h * dv, (h + 1) * dv)
        y = hid * wide(lax.rsqrt(mean_sq + LN_EPS), dv) * ng_ref[:, sl]
        h_ref[:, sl] = (y * _sigmoid(og_ref[:, sl].astype(F32))).astype(h_ref.dtype)

        m_new = m_t[L - 1:L, :]
        b_last = a[L - 1:L, :]
        w_c = jnp.exp(b_last + m_prev - m_new)
        w_s = jnp.exp(b_last - a + li - m_new)
        c_ref[h] = w_c * c_old + _dg((vh.astype(F32) * wide(w_s, dv)).astype(BF16), kh, TN)
        n_ref[h:h + 1, :] = w_c * n_old + jnp.sum(kh.astype(F32) * w_s, axis=0, keepdims=True)
        m_ref[h:h + 1, :] = m_new[:, 0:1]


def _mlstm(qkv, gates, norm_g, c0, n0, m0, *, chunk, row0, seq_len, tail=None):
    width = qkv.shape[1]
    d = width // 3
    nb, heads, dv, dqk = c0.shape
    nc = seq_len // chunk
    base = row0 // chunk
    n_main = nb * nc

    def row(s):
        return base + jnp.minimum(s, n_main - 1)

    def seq(s):
        return jnp.minimum(s // nc, nb - 1)

    state_specs = [pl.BlockSpec((None, heads, dv, dqk), lambda s: (seq(s), 0, 0, 0)),
                   pl.BlockSpec((None, heads, dqk), lambda s: (seq(s), 0, 0)),
                   pl.BlockSpec((None, heads, 1), lambda s: (seq(s), 0, 0))]
    in_specs = [pl.BlockSpec((chunk, width), lambda s: (row(s), 0)),
                pl.BlockSpec((chunk, 128), lambda s: (row(s), 0)),
                pl.BlockSpec((1, d), lambda s: (0, 0))] + state_specs
    args = [qkv, gates, norm_g, c0, n0, m0.reshape(nb, heads, 1)]
    if tail is not None:
        assert tail.shape == (chunk, d)
        in_specs.append(pl.BlockSpec((chunk, d), lambda s: (0, 0)))
        args.append(tail)
    n_steps = n_main + (tail is not None)
    outs = pl.pallas_call(
        functools.partial(_chunked_mixer_kernel, functools.partial(_mlstm_step, heads=heads), 6, 4, nc, n_main),
        grid=(n_steps,), in_specs=in_specs,
        out_specs=[pl.BlockSpec((chunk, d), lambda s: (s, 0))] + state_specs,
        out_shape=[jax.ShapeDtypeStruct((n_steps * chunk, d), BF16), jax.ShapeDtypeStruct(c0.shape, F32),
                   jax.ShapeDtypeStruct(n0.shape, F32), jax.ShapeDtypeStruct((nb, heads, 1), F32)],
        compiler_params=_params("arbitrary"), name="mlstm")(*args)
    return outs[0], outs[1], outs[2], outs[3].reshape(nb, heads)


def _retention_step(qkvg_ref, gg_ref, gb_ref, s0_ref, o_ref, s_ref, *, first, heads):
    d = qkvg_ref.shape[1] // 6
    q_ref, k_ref = qkvg_ref.at[:, 0:d], qkvg_ref.at[:, d:2 * d]
    v_ref, g_ref = qkvg_ref.at[:, 2 * d:4 * d], qkvg_ref.at[:, 4 * d:6 * d]
    L = q_ref.shape[0]
    dk = q_ref.shape[1] // heads
    dv = v_ref.shape[1] // heads

    @pl.when(first)
    def _():
        s_ref[...] = s0_ref[...]

    row = lax.broadcasted_iota(jnp.int32, (L, L), 0)
    col = lax.broadcasted_iota(jnp.int32, (L, L), 1)
    causal = col <= row
    lag = (row - col).astype(F32)
    t = lax.broadcasted_iota(jnp.int32, (L, 1), 0).astype(F32)

    for h in range(heads):
        log_gamma = math.log1p(-(2.0 ** (-5.0 - h)))
        decay = jnp.where(causal, jnp.exp(lag * log_gamma), 0.0)
        decay_q = jnp.exp((t + 1.0) * log_gamma)
        decay_k = jnp.exp((L - 1.0 - t) * log_gamma)
        decay_s = math.exp(L * log_gamma)
        qh = q_ref[:, h * dk:(h + 1) * dk]
        kh = k_ref[:, h * dk:(h + 1) * dk]
        vh = v_ref[:, h * dv:(h + 1) * dv]
        s_old = s_ref[h]
        scores = _dg(qh, kh, NT) * decay
        o = _dot(scores.astype(BF16), vh) + _dot(qh, s_old.astype(BF16)) * decay_q
        s_ref[h] = decay_s * s_old + _dg((kh.astype(F32) * decay_k).astype(BF16), vh, TN)
        sl = slice(h * dv, (h + 1) * dv)
        y = _layer_norm(o, gg_ref[:, sl], gb_ref[:, sl])
        gate = g_ref[:, sl].astype(F32)
        o_ref[:, sl] = (gate * _sigmoid(gate) * y).astype(o_ref.dtype)


def _retention(qkvg, gn_g, gn_b, s0, *, chunk, row0, seq_len, tail=None):
    width = qkvg.shape[1]
    d = width // 6
    nb, heads, dk, dv = s0.shape
    nc = seq_len // chunk
    base = row0 // chunk
    n_main = nb * nc
    state_spec = pl.BlockSpec((None, heads, dk, dv), lambda s: (jnp.minimum(s // nc, nb - 1), 0, 0, 0))
    in_specs = [pl.BlockSpec((chunk, width), lambda s: (base + jnp.minimum(s, n_main - 1), 0)),
                pl.BlockSpec((1, 2 * d), lambda s: (0, 0)),
                pl.BlockSpec((1, 2 * d), lambda s: (0, 0)), state_spec]
    args = [qkvg, gn_g, gn_b, s0]
    if tail is not None:
        assert tail.shape == (chunk, 2 * d)
        in_specs.append(pl.BlockSpec((chunk, 2 * d), lambda s: (0, 0)))
        args.append(tail)
    n_steps = n_main + (tail is not None)
    return pl.pallas_call(
        functools.partial(_chunked_mixer_kernel, functools.partial(_retention_step, heads=heads), 4, 2, nc, n_main),
        grid=(n_steps,), in_specs=in_specs,
        out_specs=[pl.BlockSpec((chunk, 2 * d), lambda s: (s, 0)), state_spec],
        out_shape=[jax.ShapeDtypeStruct((n_steps * chunk, 2 * d), BF16), jax.ShapeDtypeStruct(s0.shape, F32)],
        compiler_params=_params("arbitrary"), name="retention")(*args)


def _select_lane(block, lane_index):
    lane = lax.broadcasted_iota(jnp.int32, block.shape, 1)
    return jnp.sum(jnp.where(lane == lane_index, block, 0.0), axis=1, keepdims=True)


def _bias_tail(c_col, *, query):
    hi, mid, lo = (t.astype(F32) for t in _split3(c_col))
    lane = lax.broadcasted_iota(jnp.int32, (c_col.shape[0], 128), 1)
    if query:
        tail = jnp.where(lane == 0, hi, jnp.where(lane == 1, mid, jnp.where(lane == 2, lo,
                                                                             jnp.where(lane < 6, 1.0, 0.0))))
    else:
        tail = jnp.where(lane < 3, 1.0, jnp.where(lane == 3, -hi, jnp.where(lane == 4, -mid,
                                                                             jnp.where(lane == 5, -lo, 0.0))))
    return tail.astype(BF16)


def _fox_prompt_kernel(q_ref, k_ref, v_ref, c_ref, buf_ref, o_ref, ka_scr, vt_scr, m_scr, l_scr, acc_scr, *,
                       hps, sub):
    del buf_ref
    hg = pl.program_id(1)
    qi = pl.program_id(2)
    tq = tk = q_ref.shape[0]
    seq = k_ref.shape[0]
    hd = q_ref.shape[1] // hps

    @pl.when(qi == 0)
    def _():
        eye =(lax.broadcasted_iota(jnp.int32, (hd, hd), 0)
               == lax.broadcasted_iota(jnp.int32, (hd, hd), 1)).astype(BF16)

        def build(r, carry):
            r0 = pl.multiple_of(r * tk, tk)
            cblk = c_ref[pl.ds(r0, tk), :] * LOG2E
            for j in range(hps):
                ka_scr[j, pl.ds(r0, tk), 0:hd] = k_ref[pl.ds(r0, tk), j * hd:(j + 1) * hd]
                ka_scr[j, pl.ds(r0, tk), hd:2 * hd] = _bias_tail(_select_lane(cblk, hg * hps + j), query=False)
                vt_scr[j, :, pl.ds(r0, tk)] = _dg(eye, v_ref[pl.ds(r0, tk), j * hd:(j + 1) * hd], NT).astype(BF16)
            return carry
        lax.fori_loop(0, seq // tk, build, 0)

    cq_blk = c_ref[pl.ds(pl.multiple_of(qi * tq, tq), tq), :] * LOG2E
    qa = [jnp.concatenate([q_ref[:, j * hd:(j + 1) * hd],
                           _bias_tail(_select_lane(cq_blk, hg * hps + j), query=True)], axis=1)
          for j in range(hps)]
    m_scr[...] = jnp.full_like(m_scr, -jnp.inf)
    l_scr[...] = jnp.zeros_like(l_scr)
    acc_scr[...] = jnp.zeros_like(acc_scr)

    def logits(j, ks):
        return _dg(ka_scr[j, pl.ds(ks, sub), :], qa[j], NT)

    def absorb(j, st, ks, diag_offset):
        if diag_offset is not None:
            key = lax.broadcasted_iota(jnp.int32, st.shape, 0)
            qry = lax.broadcasted_iota(jnp.int32, st.shape, 1)
            st = jnp.where(key + diag_offset <= qry, st, -jnp.inf)
        m_old = m_scr[j]
        m_new = jnp.maximum(m_old, jnp.max(st, axis=0, keepdims=True))
        pt = jnp.exp2(st - m_new)
        alpha = jnp.exp2(m_old - m_new)
        l_scr[j] = alpha * l_scr[j] + jnp.sum(pt, axis=0, keepdims=True)
        acc_scr[j] = alpha * acc_scr[j] + _dot(vt_scr[j, :, pl.ds(ks, sub)], pt.astype(BF16))
        m_scr[j] = m_new

    def key_block(k0, st_cur, masked):
        units = [(j, d) for d in range(tk // sub) for j in range(hps)]
        start = lambda d: pl.multiple_of(k0 + d * sub, sub)
        for n, (j, d) in enumerate(units):
            if n + 1 < len(units):
                st_next = logits(units[n + 1][0], start(units[n + 1][1]))
            else:
                st_next = None if masked else logits(units[0][0], pl.multiple_of(k0 + tk, sub))
            absorb(j, st_cur, start(d), d * sub if masked else None)
            st_cur = st_next
        return st_cur

    def below_diagonal(kk, st):
        return key_block(pl.multiple_of(kk * tk, tk), st, False)

    st_diag = lax.fori_loop(0, qi, below_diagonal, logits(0, 0))
    key_block(pl.multiple_of(qi * tk, tk), st_diag, True)
    for j in range(hps):
        o_ref[:, j * hd:(j + 1) * hd] = jnp.transpose(acc_scr[j] / l_scr[j]).astype(o_ref.dtype)


def _fox_prompt(qkv, c, *, heads, n_seq, seq_len, tq, hps, sub):
    m, width = qkv.shape
    d = width // 3
    hd = d // heads
    nq = seq_len // tq
    ng = heads // hps
    return pl.pallas_call(
        functools.partial(_fox_prompt_kernel, hps=hps, sub=sub), grid=(n_seq, ng, nq),
        in_specs=[pl.BlockSpec((tq, hps * hd), lambda b, g, i: (b * nq + i, g)),
                  pl.BlockSpec((seq_len, hps * hd), lambda b, g, i: (b, ng + g), pipeline_mode=pl.Buffered(1)),
                  pl.BlockSpec((seq_len, hps * hd), lambda b, g, i: (b, 2 * ng + g), pipeline_mode=pl.Buffered(1)),
                  pl.BlockSpec((seq_len, 128), lambda b, g, i: (b, 0), pipeline_mode=pl.Buffered(1)),
                  pl.BlockSpec(memory_space=pl.ANY)],
        out_specs=pl.BlockSpec((tq, hps * hd), lambda b, g, i: (b * nq + i, g)),
        out_shape=jax.ShapeDtypeStruct((m, d), BF16), input_output_aliases={4: 0},
        scratch_shapes=[pltpu.VMEM((hps, seq_len, 2 * hd), BF16), pltpu.VMEM((hps, hd, seq_len), BF16),
                        pltpu.VMEM((hps, 1, tq), F32), pltpu.VMEM((hps, 1, tq), F32),
                        pltpu.VMEM((hps, hd, tq), F32)],
        compiler_params=_params("arbitrary", "arbitrary", "arbitrary"), name="fox_prompt")(
            qkv, qkv, qkv, c, jnp.zeros((m, d), BF16))


def _fox_sample_kernel(q_ref, kn_ref, vn_ref, c_ref, kc_ref, vc_ref, lfc_ref, prev_ref, o_ref, d_scr, ct_scr):
    del prev_ref
    g = pl.program_id(1)
    past, hpg, hd = kc_ref.shape
    ts = q_ref.shape[0]

    @pl.when(g == 0)
    def _():
        later = (lax.broadcasted_iota(jnp.int32, (past, past), 0)
                 > lax.broadcasted_iota(jnp.int32, (past, past), 1)).astype(BF16)
        hi, mid, lo = _split3(lfc_ref[...])
        d_scr[...] = _dot(hi, later) + _dot(mid, later) + _dot(lo, later)
        ct_scr[...] = _transpose_exact(c_ref[...])

    c_blk = c_ref[...]
    row = lax.broadcasted_iota(jnp.int32, (ts, ts), 0)
    col = lax.broadcasted_iota(jnp.int32, (ts, ts), 1)
    for j in range(hpg):
        h = g * hpg + j
        cq = _select_lane(c_blk, h)
        d_row = d_scr[pl.ds(h, 1), :]
        cn_row = ct_scr[pl.ds(h, 1), :]
        sl = slice(j * hd, (j + 1) * hd)
        q = q_ref[:, sl]
        s_past = _dg(q, kc_ref[:, j, :].astype(BF16), NT) + (cq + d_row) * LOG2E
        s_new = _dg(q, kn_ref[:, sl], NT) + (cq - cn_row) * LOG2E
        s_new = jnp.where(col <= row, s_new, -jnp.inf)
        m = jnp.maximum(jnp.max(s_past, axis=1, keepdims=True), jnp.max(s_new, axis=1, keepdims=True))
        p_past = jnp.exp2(s_past - m)
        p_new = jnp.exp2(s_new - m)
        den = jnp.sum(p_past, axis=1, keepdims=True) + jnp.sum(p_new, axis=1, keepdims=True)
        o = _dot(p_past.astype(BF16), vc_ref[:, j, :].astype(BF16)) + _dot(p_new.astype(BF16), vn_ref[:, sl])
        o_ref[:, sl] = (o / den).astype(o_ref.dtype)


def _fox_sample(qkv, c, k_cache, v_cache, lf_cache_t, prev_out, *, heads, row0, ts, hpg=8):
    m, width = qkv.shape
    d = width // 3
    hd = d // heads
    n_streams, past = k_cache.shape[:2]
    base = row0 // ts
    ng = heads // hpg
    return pl.pallas_call(
        _fox_sample_kernel, grid=(n_streams, ng),
        in_specs=[pl.BlockSpec((ts, hpg * hd), lambda b, g: (base + b, g)),
                  pl.BlockSpec((ts, hpg * hd), lambda b, g: (base + b, ng + g)),
                  pl.BlockSpec((ts, hpg * hd), lambda b, g: (base + b, 2 * ng + g)),
                  pl.BlockSpec((ts, 128), lambda b, g: (base + b, 0)),
                  pl.BlockSpec((None, past, hpg, hd), lambda b, g: (b, 0, g, 0)),
                  pl.BlockSpec((None, past, hpg, hd), lambda b, g: (b, 0, g, 0)),
                  pl.BlockSpec((None, heads, past), lambda b, g: (b, 0, 0)),
                  pl.BlockSpec(memory_space=pl.ANY)],
        out_specs=pl.BlockSpec((ts, hpg * hd), lambda b, g: (base + b, g)),
        out_shape=jax.ShapeDtypeStruct((m, d), BF16),
        scratch_shapes=[pltpu.VMEM((heads, past), F32), pltpu.VMEM((128, ts), F32)],
        input_output_aliases={7: 0},
        compiler_params=_params("arbitrary", "arbitrary"), name="fox_sample")(
            qkv, qkv, qkv, c, k_cache, v_cache, lf_cache_t, prev_out)


def _pad_cols(w, n):
    return jnp.pad(w, ((0, 0), (0, n - w.shape[1])))


def _row_vec(v, n=None):
    v = v.astype(F32).reshape(1, -1)
    return v if n is None else _pad_cols(v, n)


def _block_diag(blocks, copies):
    g, p, _ = blocks.shape
    eye = jnp.eye(copies, dtype=blocks.dtype)
    return jnp.einsum("ab,gpq->gapbq", eye, blocks).reshape(g, copies * p, copies * p)


def _gmlp_spatial(w_s, b_s, block, group_dim):
    pos = jnp.arange(block)
    mask = (pos[None, :] // GMLP_CHUNK) <= (pos[:, None] // GMLP_CHUNK)
    ws = jnp.where(mask[None], w_s[:, :block, :block], 0.0)
    copies = ROW_TILE // block
    bias = jnp.repeat(jnp.tile(jnp.transpose(b_s[:, :block]), (copies, 1)), group_dim, axis=1)
    return _block_diag(ws, copies).astype(BF16), bias.astype(F32)


def _cumsum_matrix(block):
    r = jnp.arange(ROW_TILE)
    return ((r[None, :] <= r[:, None]) & (r[None, :] // block == r[:, None] // block)).astype(BF16)


def kernel(x_prompt, x_sample, state_b_C, state_b_n, state_b_m, state_c_S, cache_d_k, cache_d_v, cache_d_logf, a_w_in, a_b_in, a_vn_g, a_vn_b, a_w_s, a_b_s, a_w_out, b_w_in, b_b_gates, b_norm_g, b_w_out, c_w_in, c_gn_g, c_gn_b, c_w_out, d_w_in, d_b_f, d_w_out, ffn_w_in, ffn_w_out, ln1_g, ln1_b, ln2_g, ln2_b):
    nbp, seq, d = x_prompt.shape
    nbs, ts, _ = x_sample.shape
    mp, ms = nbp * seq, nbs * ts
    past = cache_d_k.shape[2]
    depth = ffn_w_in.shape[0]
    alpha = (2.0 * depth) ** 0.25
    assert ms == ROW_TILE == SEQ_CHUNK and ROW_TILE % ts == 0 and seq % ATTN_BLOCK == 0 and seq % SEQ_CHUNK == 0
    n_prompt_tiles = mp // ROW_TILE

    x = jnp.concatenate([x_prompt.reshape(mp, d), x_sample.reshape(ms, d)], axis=0)
    ffn_w_in_b, ffn_w_out_b = ffn_w_in.astype(BF16), ffn_w_out.astype(BF16)

    a_vs = []
    b_cp, b_np, b_mp, b_cs, b_ns, b_ms = [], [], [], [], [], []
    c_sp, c_ss = [], []
    d_kp, d_vp, d_fp, d_ks, d_vs, d_fs = [], [], [], [], [], []

    for i in range(depth):
        kind, j = i % 4, i // 4
        if kind == 0:
            w_in = a_w_in[j].astype(BF16)
            u, v, v_last = _gmlp_in(x, w_in[:, :d], _row_vec(a_b_in[j, :d]), w_in[:, d:], _row_vec(a_b_in[j, d:]),
                                    _row_vec(a_vn_g[j]), _row_vec(a_vn_b[j]))
            gd = d // GMLP_GROUPS
            wp, bp = _gmlp_spatial(a_w_s[j], a_b_s[j], GMLP_BLOCK, gd)
            wsm, bsm = _gmlp_spatial(a_w_s[j], a_b_s[j], ts, gd)
            mix = _gmlp_gate(u, v, jnp.stack([wp, wsm]), jnp.stack([bp, bsm]), n_prompt_tiles)
            w_out = a_w_out[j]
            a_vs.append(v_last[v_last.shape[0] - ms:].reshape(nbs, ts, d))
        elif kind == 1:
            heads = MLSTM_HEADS
            n_main = b_w_in.shape[2] - 2 * heads
            dqk = (n_main - 2 * d) // (2 * heads)
            qkv, gates = _qkv_proj(x, b_w_in[j, :, :n_main].astype(BF16),
                                   _pad_cols(b_w_in[j, :, n_main:], 128).astype(BF16), _row_vec(b_b_gates[j], 128),
                                   scaled_cols=heads * dqk, scale=dqk ** -0.5, n_raw=heads, tn=PROJ_WIDE_COLS)
            norm_g = _row_vec(b_norm_g[j])
            zc = jnp.zeros((nbp,) + state_b_C.shape[2:], F32)
            zn = jnp.zeros((nbp,) + state_b_n.shape[2:], F32)
            zm = jnp.zeros((nbp,) + state_b_m.shape[2:], F32)
            mix_s, cs, ns, ms_state = _mlstm(qkv, gates, norm_g, state_b_C[j], state_b_n[j], state_b_m[j],
                                             chunk=ts, row0=mp, seq_len=ts)
            mix, cp, np_, mp_state = _mlstm(qkv, gates, norm_g, zc, zn, zm, chunk=SEQ_CHUNK, row0=0, seq_len=seq,
                                            tail=mix_s)
            b_cp.append(cp); b_np.append(np_); b_mp.append(mp_state)
            b_cs.append(cs); b_ns.append(ns); b_ms.append(ms_state)
            w_out = b_w_out[j]
        elif kind == 2:
            heads = RET_HEADS
            dk = d // heads
            half = dk // 2
            inv = ROPE_BASE ** (-jnp.arange(half, dtype=F32) / half)
            pos = jnp.concatenate([jnp.tile(jnp.arange(seq), nbp), past + jnp.tile(jnp.arange(ts), nbs)])
            ang = pos.astype(F32)[:, None] * inv[None, :]
            qkvg = _rot_proj(x, c_w_in[j].astype(BF16), jnp.cos(ang), jnp.sin(ang),
                             n_q=d // PROJ_WIDE_COLS, n_k=d // PROJ_WIDE_COLS, k_scale=dk ** -0.5, head=dk,
                             tn=PROJ_WIDE_COLS)
            gn_g, gn_b = _row_vec(c_gn_g[j]), _row_vec(c_gn_b[j])
            zs = jnp.zeros((nbp,) + state_c_S.shape[2:], F32)
            mix_s, ss = _retention(qkvg, gn_g, gn_b, state_c_S[j], chunk=ts, row0=mp, seq_len=ts)
            mix, sp = _retention(qkvg, gn_g, gn_b, zs, chunk=SEQ_CHUNK, row0=0, seq_len=seq, tail=mix_s)
            c_sp.append(sp); c_ss.append(ss)
            w_out = c_w_out[j]
        else:
            heads = FOX_HEADS
            hd = d // heads
            qkv, logf, k_p, k_s, v_p, v_s = _qkv_proj(
                x, d_w_in[j, :, :3 * d].astype(BF16), _pad_cols(d_w_in[j, :, 3 * d:], 128).astype(BF16),
                _row_vec(d_b_f[j], 128), scaled_cols=d, scale=hd ** -0.5 * LOG2E, n_raw=0, kv_rows=(mp, ms))
            tri = jnp.stack([_cumsum_matrix(ROW_TILE), _cumsum_matrix(ts)])
            c = _running_sum(logf, tri, seq // ROW_TILE, n_prompt_tiles)
            mix = _fox_prompt(qkv, c, heads=heads, n_seq=nbp, seq_len=seq, tq=ATTN_BLOCK, hps=ATTN_HEADS_PER_STEP,
                              sub=ATTN_KEY_SUB)
            mix = _fox_sample(qkv, c, cache_d_k[j], cache_d_v[j],
                              jnp.transpose(cache_d_logf[j], (0, 2, 1)), mix, heads=heads, row0=mp, ts=ts)
            d_kp.append(k_p.reshape(nbp, seq, heads, hd)); d_ks.append(k_s.reshape(nbs, ts, heads, hd))
            d_vp.append(v_p.reshape(nbp, seq, heads, hd)); d_vs.append(v_s.reshape(nbs, ts, heads, hd))
            d_fp.append(logf[:mp, :heads].reshape(nbp, seq, heads))
            d_fs.append(logf[mp:, :heads].reshape(nbs, ts, heads))
            w_out = d_w_out[j]
        x = _out_ln(mix, w_out.astype(BF16), x, _row_vec(ln1_g[i]), _row_vec(ln1_b[i]), alpha)
        x = _ffn(x, ffn_w_in_b, ffn_w_out_b, i, _row_vec(ln2_g[i]), _row_vec(ln2_b[i]), alpha,
                 final_rows=(mp, ms) if i == depth - 1 else None)

    y_prompt, y_sample = x
    return (y_prompt.reshape(nbp, seq, d), y_sample.reshape(nbs, ts, d), jnp.stack(a_vs),
            jnp.stack(b_cp), jnp.stack(b_np), jnp.stack(b_mp),
            jnp.stack(b_cs), jnp.stack(b_ns), jnp.stack(b_ms),
            jnp.stack(c_sp), jnp.stack(c_ss),
            jnp.stack(d_kp), jnp.stack(d_vp), jnp.stack(d_fp),
            jnp.stack(d_ks), jnp.stack(d_vs), jnp.stack(d_fs))
```

```python
import functools
import math

import jax
import jax.numpy as jnp
from jax import lax
from jax.experimental import pallas as pl
from jax.experimental.pallas import tpu as pltpu

F32 = jnp.float32
BF16 = jnp.bfloat16

LN_EPS = 1e-5
ROPE_BASE = 10000.0
GMLP_BLOCK = 128
GMLP_CHUNK = 64
GMLP_GROUPS = 8
MLSTM_HEADS = 8
RET_HEADS = 8
FOX_HEADS = 16
LANES = 128
ROW_TILE = 256
SEQ_CHUNK = 256
ATTN_BLOCK = 512
ATTN_HEADS_PER_STEP = 4
PROJ_WIDE_COLS = 2048
ATTN_KEY_SUB = 256
LOG2E = math.log2(math.e)
VMEM_LIMIT = 56 * 2 ** 20

NT = (((1,), (1,)), ((), ()))
TN = (((0,), (0,)), ((), ()))


def _params(*sem):
    return pltpu.CompilerParams(dimension_semantics=sem, vmem_limit_bytes=VMEM_LIMIT)


def _row_tile(m, candidates):
    for t in candidates:
        if m % t == 0:
            return t
    raise ValueError(f"no row tile for {m}")


def _dot(a, b):
    return jnp.dot(a, b, preferred_element_type=F32)


def _dg(a, b, dims):
    return lax.dot_general(a, b, dims, preferred_element_type=F32)


def _split3(x):
    hi = x.astype(BF16)
    r1 = x - hi.astype(F32)
    mid = r1.astype(BF16)
    lo = (r1 - mid.astype(F32)).astype(BF16)
    return hi, mid, lo


def _dot_exact_lhs(mat_bf16, x):
    hi, mid, lo = _split3(x)
    return _dot(mat_bf16, hi) + _dot(mat_bf16, mid) + _dot(mat_bf16, lo)


def _transpose_exact(x):
    n = x.shape[1]
    eye = (lax.broadcasted_iota(jnp.int32, (n, n), 0) == lax.broadcasted_iota(jnp.int32, (n, n), 1)).astype(BF16)
    hi, mid, lo = _split3(x)
    return _dg(eye, hi, NT) + _dg(eye, mid, NT) + _dg(eye, lo, NT)


def _sigmoid(x):
    return 1.0 / (1.0 + jnp.exp(-x))


def _log_sigmoid(x):
    return jnp.minimum(x, 0.0) - jnp.log1p(jnp.exp(-jnp.abs(x)))


def _gelu_tanh(x):
    return 0.5 * x * (1.0 + jnp.tanh(math.sqrt(2.0 / math.pi) * (x + 0.044715 * (x * x * x))))


def _layer_norm(y, g, b):
    mu = jnp.mean(y, axis=-1, keepdims=True)
    d = y - mu
    var = jnp.mean(d * d, axis=-1, keepdims=True)
    return d * lax.rsqrt(var + LN_EPS) * g + b


def _matmul_row_parts(lhs_ref, w_ref, epilogue, cast=False):
    tm = lhs_ref.shape[0]
    n_parts = 3 if tm % 48 == 0 else 2
    part = tm // n_parts
    rows = [slice(p * part, (p + 1) * part) for p in range(n_parts)]

    def matmul(r):
        lhs = lhs_ref[r, :]
        return _dot(lhs.astype(BF16) if cast else lhs, w_ref[...])

    acc = matmul(rows[0])
    for p, r in enumerate(rows):
        nxt = matmul(rows[p + 1]) if p + 1 < n_parts else None
        epilogue(r, acc)
        acc = nxt


def _cast_rows_once(x_ref, xb_scr, j):
    @pl.when(j == 0)
    def _():
        xb_scr[...] = x_ref[...].astype(BF16)


def _gelu_proj_kernel(x_ref, w_ref, b_ref, o_ref):
    def epilogue(r, acc):
        o_ref[r, :] = _gelu_tanh(acc + b_ref[...]).astype(o_ref.dtype)

    _matmul_row_parts(x_ref, w_ref, epilogue, cast=True)


def _gelu_ln_proj_kernel(x_ref, w_ref, b_ref, g_ref, be_ref, o_ref, last_ref):
    def epilogue(r, acc):
        v = _layer_norm(_gelu_tanh(acc + b_ref[...]), g_ref[...], be_ref[...])
        o_ref[r, :] = v.astype(o_ref.dtype)
        last_ref[r, :] = v

    _matmul_row_parts(x_ref, w_ref, epilogue, cast=True)


def _gmlp_in(x, w_u, b_u, w_v, b_v, vn_g, vn_b):
    m, d = x.shape
    tm = _row_tile(m, (384, 256))
    n = w_u.shape[1]
    row = pl.BlockSpec((tm, d), lambda i: (i, 0))
    wsp = pl.BlockSpec((d, n), lambda i: (0, 0))
    vec = pl.BlockSpec((1, n), lambda i: (0, 0))
    out = pl.BlockSpec((tm, n), lambda i: (i, 0))
    u = pl.pallas_call(
        _gelu_proj_kernel, grid=(m // tm,), in_specs=[row, wsp, vec], out_specs=out,
        out_shape=jax.ShapeDtypeStruct((m, n), BF16), compiler_params=_params("arbitrary"),
        name="gmlp_u")(x, w_u, b_u)
    v, v_last = pl.pallas_call(
        _gelu_ln_proj_kernel, grid=(m // tm,), in_specs=[row, wsp, vec, vec, vec],
        out_specs=[out, pl.BlockSpec((tm, n), lambda i: (0, 0))],
        out_shape=[jax.ShapeDtypeStruct((m, n), BF16), jax.ShapeDtypeStruct((tm, n), F32)],
        compiler_params=_params("arbitrary"), name="gmlp_v")(x, w_v, b_v, vn_g, vn_b)
    return u, v, v_last


def _qkv_proj_kernel(x_ref, w_ref, wg_ref, bg_ref, o_ref, gate_ref, *refs, scaled_cols, scale, nb, tail, n_raw):
    kv_refs, xb_scr = refs[:-1], refs[-1]
    j = pl.program_id(1)
    _cast_rows_once(x_ref, xb_scr, j)

    @pl.when(j == 0)
    def _():
        g = _dot(xb_scr[...], wg_ref[...]) + bg_ref[...]
        lane = lax.broadcasted_iota(jnp.int32, g.shape, 1)
        gate_ref[...] = jnp.where(lane < n_raw, g, _log_sigmoid(g))

    tn = o_ref.shape[1]
    col = j * tn + lax.broadcasted_iota(jnp.int32, (1, tn), 1)
    col_scale = jnp.where(col < scaled_cols, scale, 1.0)
    accs = []

    def epilogue(r, acc):
        o_ref[r, :] = (acc * col_scale).astype(o_ref.dtype)
        accs.append(acc)

    _matmul_row_parts(xb_scr, w_ref, epilogue)

    if kv_refs:
        kp_ref, ks_ref, vp_ref, vs_ref = kv_refs
        tm, tn = o_ref.shape
        acc = jnp.concatenate(accs, axis=0)

        @pl.when(jnp.logical_and(j >= nb, j < 2 * nb))
        def _():
            kp_ref[...] = acc
            ks_ref[:, pl.ds(pl.multiple_of((j - nb) * tn, tn), tn)] = acc[tm - tail:]

        @pl.when(j >= 2 * nb)
        def _():
            vp_ref[...] = acc
            vs_ref[:, pl.ds(pl.multiple_of((j - 2 * nb) * tn, tn), tn)] = acc[tm - tail:]


def _qkv_proj(x, w, w_gate, b_gate, *, scaled_cols, scale, n_raw, kv_rows=None, tn=1024):
    m, d = x.shape
    n = w.shape[1]
    tm = _row_tile(m, (768, 512, 256))
    in_specs = [pl.BlockSpec((tm, d), lambda i, j: (i, 0)), pl.BlockSpec((d, tn), lambda i, j: (0, j)),
                pl.BlockSpec((d, 128), lambda i, j: (0, 0)), pl.BlockSpec((1, 128), lambda i, j: (0, 0))]
    out_specs = [pl.BlockSpec((tm, tn), lambda i, j: (i, j)), pl.BlockSpec((tm, 128), lambda i, j: (i, 0))]
    out_shape = [jax.ShapeDtypeStruct((m, n), BF16), jax.ShapeDtypeStruct((m, 128), F32)]
    nb, tail = n // (3 * tn), 0
    if kv_rows is not None:
        mp, tail = kv_rows
        assert mp + tail == m and tail <= tm
        for first in (nb, 2 * nb):
            col = functools.partial(lambda j, first: jnp.clip(j - first, 0, nb - 1), first=first)
            out_specs += [pl.BlockSpec((tm, tn), lambda i, j, col=col: (i, col(j))),
                          pl.BlockSpec((tail, n // 3), lambda i, j: (0, 0))]
            out_shape += [jax.ShapeDtypeStruct((mp, n // 3), F32), jax.ShapeDtypeStruct((tail, n // 3), F32)]
    return pl.pallas_call(
        functools.partial(_qkv_proj_kernel, scaled_cols=scaled_cols, scale=scale, nb=nb, tail=tail, n_raw=n_raw),
        grid=(m // tm, n // tn), in_specs=in_specs, out_specs=out_specs, out_shape=out_shape,
        scratch_shapes=[pltpu.VMEM((tm, d), BF16)],
        compiler_params=_params("arbitrary", "arbitrary"), name="qkv_proj")(x, w, w_gate, b_gate)


def _rot_proj_kernel(x_ref, w_ref, cos_ref, sin_ref, o_ref, xb_scr, *, n_q, n_k, k_scale, head):
    j = pl.program_id(1)
    _cast_rows_once(x_ref, xb_scr, j)
    half = head // 2
    rotated = j < n_q + n_k
    col_scale = jnp.where(jnp.logical_and(j >= n_q, rotated), k_scale, 1.0).astype(F32)

    def epilogue(r, acc):
        cos = jnp.where(rotated, cos_ref[r, :], 1.0)
        sin = jnp.where(rotated, sin_ref[r, :], 0.0)
        for h in range(acc.shape[1] // head):
            x1 = acc[:, h * head:h * head + half]
            x2 = acc[:, h * head + half:(h + 1) * head]
            o_ref[r, h * head:h * head + half] = ((x1 * cos - x2 * sin) * col_scale).astype(o_ref.dtype)
            o_ref[r, h * head + half:(h + 1) * head] = ((x1 * sin + x2 * cos) * col_scale).astype(o_ref.dtype)

    _matmul_row_parts(xb_scr, w_ref, epilogue)


def _rot_proj(x, w, cos, sin, *, n_q, n_k, k_scale, head, tn=1024):
    m, d = x.shape
    n = w.shape[1]
    tm = _row_tile(m, (768, 512, 256))
    half = head // 2
    return pl.pallas_call(
        functools.partial(_rot_proj_kernel, n_q=n_q, n_k=n_k, k_scale=k_scale, head=head),
        grid=(m // tm, n // tn),
        in_specs=[pl.BlockSpec((tm, d), lambda i, j: (i, 0)), pl.BlockSpec((d, tn), lambda i, j: (0, j)),
                  pl.BlockSpec((tm, half), lambda i, j: (i, 0)), pl.BlockSpec((tm, half), lambda i, j: (i, 0))],
        out_specs=pl.BlockSpec((tm, tn), lambda i, j: (i, j)),
        out_shape=jax.ShapeDtypeStruct((m, n), BF16), scratch_shapes=[pltpu.VMEM((tm, d), BF16)],
        compiler_params=_params("arbitrary", "arbitrary"), name="rot_proj")(x, w, cos, sin)


def _running_sum_kernel(lf_ref, tri_ref, c_ref, carry_ref, *, tiles_per_seq):
    i = pl.program_id(0)

    @pl.when(i % tiles_per_seq == 0)
    def _():
        carry_ref[...] = jnp.zeros_like(carry_ref)

    c = _dot_exact_lhs(tri_ref[...], lf_ref[...]) + carry_ref[...]
    c_ref[...] = c
    carry_ref[...] = c[-1:, :]


def _running_sum(lf, tri, tiles_per_seq, n_prompt_tiles):
    m = lf.shape[0]
    tm = ROW_TILE
    blk = pl.BlockSpec((tm, 128), lambda i: (i, 0))
    return pl.pallas_call(
        functools.partial(_running_sum_kernel, tiles_per_seq=tiles_per_seq), grid=(m // tm,),
        in_specs=[blk, pl.BlockSpec((None, tm, tm), lambda i: (i // n_prompt_tiles, 0, 0))],
        out_specs=blk, out_shape=jax.ShapeDtypeStruct((m, 128), F32),
        scratch_shapes=[pltpu.VMEM((1, 128), F32)],
        compiler_params=_params("arbitrary"), name="running_sum")(lf, tri)


def _out_ln_kernel(h_ref, w_ref, x_ref, g_ref, b_ref, o_ref, *, alpha):
    def epilogue(r, acc):
        o_ref[r, :] = _layer_norm(alpha * x_ref[r, :] + acc, g_ref[...], b_ref[...])

    _matmul_row_parts(h_ref, w_ref, epilogue)


def _out_ln(h, w, x, g, b, alpha):
    m, k = h.shape
    d = w.shape[1]
    tm = _row_tile(m, (768, 384, 256) if k <= d else (384, 256))
    vec = pl.BlockSpec((1, d), lambda i: (0, 0))
    row = pl.BlockSpec((tm, d), lambda i: (i, 0))
    return pl.pallas_call(
        functools.partial(_out_ln_kernel, alpha=alpha), grid=(m // tm,),
        in_specs=[pl.BlockSpec((tm, k), lambda i: (i, 0)),
                  pl.BlockSpec((k, d), lambda i: (0, 0), pipeline_mode=pl.Buffered(1)),
                  row, vec, vec],
        out_specs=row, out_shape=jax.ShapeDtypeStruct((m, d), F32),
        compiler_params=_params("arbitrary"), name="out_ln")(h, w, x, g, b)


def _ffn_kernel(x_ref, wg_ref, wu_ref, wo_ref, g_ref, b_ref, o_ref, *rest, alpha):
    tail_refs, xb_scr = rest[:-1], rest[-1]
    j = pl.program_id(1)
    _cast_rows_once(x_ref, xb_scr, j)

    @pl.when(j == 0)
    def _():
        o_ref[...] = jnp.zeros_like(o_ref)

    xb = xb_scr[...]
    gate = _dot(xb, wg_ref[...])
    up = _dot(xb, wu_ref[...])
    hid = (gate * _sigmoid(gate) * up).astype(BF16)
    o_ref[...] += _dot(hid, wo_ref[...])

    @pl.when(j == pl.num_programs(1) - 1)
    def _():
        y = _layer_norm(alpha * x_ref[...] + o_ref[...], g_ref[...], b_ref[...])
        o_ref[...] = y
        for t_ref in tail_refs:
            t_ref[...] = y[y.shape[0] - t_ref.shape[0]:]


def _ffn(x, w_in, w_out, layer, g, b, alpha, final_rows=None, th=512):
    m, d = x.shape
    hidden = w_out.shape[1]
    tm = _row_tile(m, (768, 512, 256))
    nh = hidden // th
    vec = pl.BlockSpec((1, d), lambda i, j: (0, 0))
    row = pl.BlockSpec((tm, d), lambda i, j: (i, 0))
    if final_rows is None:
        out_specs, out_shape = row, jax.ShapeDtypeStruct((m, d), F32)
    else:
        mp, tail = final_rows
        assert mp + tail == m and tail <= tm
        out_specs = [row, pl.BlockSpec((tail, d), lambda i, j: (0, 0))]
        out_shape = [jax.ShapeDtypeStruct((mp, d), F32), jax.ShapeDtypeStruct((tail, d), F32)]
    return pl.pallas_call(
        functools.partial(_ffn_kernel, alpha=alpha), grid=(m // tm, nh),
        in_specs=[row, pl.BlockSpec((None, d, th), lambda i, j: (layer, 0, j)),
                  pl.BlockSpec((None, d, th), lambda i, j: (layer, 0, j + nh)),
                  pl.BlockSpec((None, th, d), lambda i, j: (layer, j, 0)), vec, vec],
        out_specs=out_specs, out_shape=out_shape,
        scratch_shapes=[pltpu.VMEM((tm, d), BF16)],
        compiler_params=_params("arbitrary", "arbitrary"), name="ffn")(x, w_in, w_in, w_out, g, b)


def _gmlp_gate_kernel(u_ref, v_ref, w_ref, b_ref, y_ref, *, groups):
    gd = u_ref.shape[1] // groups
    for g in range(groups):
        sl = slice(g * gd, (g + 1) * gd)
        s = _dot(w_ref[g], v_ref[:, sl]) + b_ref[:, sl]
        y_ref[:, sl] = (u_ref[:, sl].astype(F32) * s).astype(y_ref.dtype)


def _gmlp_gate(u, v, w_sp, b_sp, n_prompt_tiles):
    m, d = u.shape
    tm = ROW_TILE
    groups = w_sp.shape[1]
    row = pl.BlockSpec((tm, d), lambda i: (i, 0))
    return pl.pallas_call(
        functools.partial(_gmlp_gate_kernel, groups=groups), grid=(m // tm,),
        in_specs=[row, row,
                  pl.BlockSpec((None, groups, tm, tm), lambda i: (i // n_prompt_tiles, 0, 0, 0)),
                  pl.BlockSpec((None, tm, d), lambda i: (i // n_prompt_tiles, 0, 0))],
        out_specs=row, out_shape=jax.ShapeDtypeStruct((m, d), BF16),
        compiler_params=_params("arbitrary"), name="gmlp_gate")(u, v, w_sp, b_sp)


def _chunked_mixer_kernel(step_fn, n_in, n_out, n_chunks, n_main, *refs):
    ins, outs = refs[:n_in], refs[len(refs) - n_out:]
    s = pl.program_id(0)
    if len(refs) > n_in + n_out:
        @pl.when(s == n_main)
        def _():
            outs[0][...] = refs[n_in][...]

    @pl.when(s < n_main)
    def _():
        step_fn(*ins, *outs, first=(s % n_chunks == 0))


def _mlstm_step(qkv_ref, gt_ref, ng_ref, c0_ref, n0_ref, m0_ref, h_ref, c_ref, n_ref, m_ref, *, first, heads):
    d = qkv_ref.shape[1] // 3
    q_ref, k_ref = qkv_ref.at[:, 0:d // 2], qkv_ref.at[:, d // 2:d]
    v_ref, og_ref = qkv_ref.at[:, d:2 * d], qkv_ref.at[:, 2 * d:3 * d]
    L = q_ref.shape[0]
    dqk = q_ref.shape[1] // heads
    dv = v_ref.shape[1] // heads

    @pl.when(first)
    def _():
        c_ref[...] = c0_ref[...]
        n_ref[...] = n0_ref[...]
        m_ref[...] = m0_ref[...]

    g = gt_ref[...]
    row = lax.broadcasted_iota(jnp.int32, (L, L), 0)
    col = lax.broadcasted_iota(jnp.int32, (L, L), 1)
    causal = col <= row
    bc = _dot_exact_lhs(causal.astype(BF16), g)
    g_t = _transpose_exact(g)
    bc_t = _transpose_exact(bc)

    ones_cols = jnp.ones((L, LANES), BF16)
    mean_cols = jnp.full((dv, LANES), 1.0 / dv, BF16)

    def rep(col):
        return jnp.broadcast_to(col, (L, LANES))

    def wide(x_rep, cols):
        return x_rep[:, :cols] if cols < LANES else jnp.concatenate([x_rep] * (cols // LANES), axis=1)

    for h in range(heads):
        a = rep(bc[:, heads + h:heads + h + 1])
        li = rep(g[:, h:h + 1])
        a_row = bc_t[heads + h:heads + h + 1, :]
        li_row = g_t[h:h + 1, :]
        m_prev = m_ref[h:h + 1, :]
        dmat = jnp.where(causal, wide(a, L) - a_row + li_row, -jnp.inf)
        inter = a + m_prev
        m_t = jnp.maximum(inter, rep(jnp.max(dmat, axis=1, keepdims=True)))
        w_intra = jnp.exp(dmat - wide(m_t, L))
        w_inter = jnp.exp(inter - m_t)
        qh = q_ref[:, h * dqk:(h + 1) * dqk]
        kh = k_ref[:, h * dqk:(h + 1) * dqk]
        vh = v_ref[:, h * dv:(h + 1) * dv]
        c_old = c_ref[h]
        n_old = n_ref[h:h + 1, :]
        scores = (_dg(qh, kh, NT) * w_intra).astype(BF16)
        v_ones = jnp.concatenate([vh, ones_cols], axis=1)
        c_n = jnp.concatenate([c_old, jnp.broadcast_to(n_old, (LANES, dqk))], axis=0).astype(BF16)
        both = _dot(scores, v_ones) + wide(w_inter, dv + LANES) * _dg(qh, c_n, NT)
        inv = 1.0 / jnp.maximum(jnp.abs(both[:, dv:]), jnp.exp(-m_t))
        hid = both[:, :dv] * wide(inv, dv)
        mean_sq = _dot((hid * hid).astype(BF16), mean_cols)
        sl = slice(h * dv, (h + 1) * dv)
        y = hid * wide(lax.rsqrt(mean_sq + LN_EPS), dv) * ng_ref[:, sl]
        h_ref[:, sl] = (y * _sigmoid(og_ref[:, sl].astype(F32))).astype(h_ref.dtype)

        m_new = m_t[L - 1:L, :]
        b_last = a[L - 1:L, :]
        w_c = jnp.exp(b_last + m_prev - m_new)
        w_s = jnp.exp(b_last - a + li - m_new)
        c_ref[h] = w_c * c_old + _dg((vh.astype(F32) * wide(w_s, dv)).astype(BF16), kh, TN)
        n_ref[h:h + 1, :] = w_c * n_old + jnp.sum(kh.astype(F32) * w_s, axis=0, keepdims=True)
        m_ref[h:h + 1, :] = m_new[:, 0:1]


def _mlstm(qkv, gates, norm_g, c0, n0, m0, *, chunk, row0, seq_len, tail=None):
    width = qkv.shape[1]
    d = width // 3
    nb, heads, dv, dqk = c0.shape
    nc = seq_len // chunk
    base = row0 // chunk
    n_main = nb * nc

    def row(s):
        return base + jnp.minimum(s, n_main - 1)

    def seq(s):
        return jnp.minimum(s // nc, nb - 1)

    state_specs = [pl.BlockSpec((None, heads, dv, dqk), lambda s: (seq(s), 0, 0, 0)),
                   pl.BlockSpec((None, heads, dqk), lambda s: (seq(s), 0, 0)),
                   pl.BlockSpec((None, heads, 1), lambda s: (seq(s), 0, 0))]
    in_specs = [pl.BlockSpec((chunk, width), lambda s: (row(s), 0)),
                pl.BlockSpec((chunk, 128), lambda s: (row(s), 0)),
                pl.BlockSpec((1, d), lambda s: (0, 0))] + state_specs
    args = [qkv, gates, norm_g, c0, n0, m0.reshape(nb, heads, 1)]
    if tail is not None:
        assert tail.shape == (chunk, d)
        in_specs.append(pl.BlockSpec((chunk, d), lambda s: (0, 0)))
        args.append(tail)
    n_steps = n_main + (tail is not None)
    outs = pl.pallas_call(
        functools.partial(_chunked_mixer_kernel, functools.partial(_mlstm_step, heads=heads), 6, 4, nc, n_main),
        grid=(n_steps,), in_specs=in_specs,
        out_specs=[pl.BlockSpec((chunk, d), lambda s: (s, 0))] + state_specs,
        out_shape=[jax.ShapeDtypeStruct((n_steps * chunk, d), BF16), jax.ShapeDtypeStruct(c0.shape, F32),
                   jax.ShapeDtypeStruct(n0.shape, F32), jax.ShapeDtypeStruct((nb, heads, 1), F32)],
        compiler_params=_params("arbitrary"), name="mlstm")(*args)
    return outs[0], outs[1], outs[2], outs[3].reshape(nb, heads)


def _retention_step(qkvg_ref, gg_ref, gb_ref, s0_ref, o_ref, s_ref, *, first, heads):
    d = qkvg_ref.shape[1] // 6
    q_ref, k_ref = qkvg_ref.at[:, 0:d], qkvg_ref.at[:, d:2 * d]
    v_ref, g_ref = qkvg_ref.at[:, 2 * d:4 * d], qkvg_ref.at[:, 4 * d:6 * d]
    L = q_ref.shape[0]
    dk = q_ref.shape[1] // heads
    dv = v_ref.shape[1] // heads

    @pl.when(first)
    def _():
        s_ref[...] = s0_ref[...]

    row = lax.broadcasted_iota(jnp.int32, (L, L), 0)
    col = lax.broadcasted_iota(jnp.int32, (L, L), 1)
    causal = col <= row
    lag = (row - col).astype(F32)
    t = lax.broadcasted_iota(jnp.int32, (L, 1), 0).astype(F32)

    for h in range(heads):
        log_gamma = math.log1p(-(2.0 ** (-5.0 - h)))
        decay = jnp.where(causal, jnp.exp(lag * log_gamma), 0.0)
        decay_q = jnp.exp((t + 1.0) * log_gamma)
        decay_k = jnp.exp((L - 1.0 - t) * log_gamma)
        decay_s = math.exp(L * log_gamma)
        qh = q_ref[:, h * dk:(h + 1) * dk]
        kh = k_ref[:, h * dk:(h + 1) * dk]
        vh = v_ref[:, h * dv:(h + 1) * dv]
        s_old = s_ref[h]
        scores = _dg(qh, kh, NT) * decay
        o = _dot(scores.astype(BF16), vh) + _dot(qh, s_old.astype(BF16)) * decay_q
        s_ref[h] = decay_s * s_old + _dg((kh.astype(F32) * decay_k).astype(BF16), vh, TN)
        sl = slice(h * dv, (h + 1) * dv)
        y = _layer_norm(o, gg_ref[:, sl], gb_ref[:, sl])
        gate = g_ref[:, sl].astype(F32)
        o_ref[:, sl] = (gate * _sigmoid(gate) * y).astype(o_ref.dtype)


def _retention(qkvg, gn_g, gn_b, s0, *, chunk, row0, seq_len, tail=None):
    width = qkvg.shape[1]
    d = width // 6
    nb, heads, dk, dv = s0.shape
    nc = seq_len // chunk
    base = row0 // chunk
    n_main = nb * nc
    state_spec = pl.BlockSpec((None, heads, dk, dv), lambda s: (jnp.minimum(s // nc, nb - 1), 0, 0, 0))
    in_specs = [pl.BlockSpec((chunk, width), lambda s: (base + jnp.minimum(s, n_main - 1), 0)),
                pl.BlockSpec((1, 2 * d), lambda s: (0, 0)),
                pl.BlockSpec((1, 2 * d), lambda s: (0, 0)), state_spec]
    args = [qkvg, gn_g, gn_b, s0]
    if tail is not None:
        assert tail.shape == (chunk, 2 * d)
        in_specs.append(pl.BlockSpec((chunk, 2 * d), lambda s: (0, 0)))
        args.append(tail)
    n_steps = n_main + (tail is not None)
    return pl.pallas_call(
        functools.partial(_chunked_mixer_kernel, functools.partial(_retention_step, heads=heads), 4, 2, nc, n_main),
        grid=(n_steps,), in_specs=in_specs,
        out_specs=[pl.BlockSpec((chunk, 2 * d), lambda s: (s, 0)), state_spec],
        out_shape=[jax.ShapeDtypeStruct((n_steps * chunk, 2 * d), BF16), jax.ShapeDtypeStruct(s0.shape, F32)],
        compiler_params=_params("arbitrary"), name="retention")(*args)


def _select_lane(block, lane_index):
    lane = lax.broadcasted_iota(jnp.int32, block.shape, 1)
    return jnp.sum(jnp.where(lane == lane_index, block, 0.0), axis=1, keepdims=True)


def _bias_tail(c_col, *, query):
    hi, mid, lo = (t.astype(F32) for t in _split3(c_col))
    lane = lax.broadcasted_iota(jnp.int32, (c_col.shape[0], 128), 1)
    if query:
        tail = jnp.where(lane == 0, hi, jnp.where(lane == 1, mid, jnp.where(lane == 2, lo,
                                                                             jnp.where(lane < 6, 1.0, 0.0))))
    else:
        tail = jnp.where(lane < 3, 1.0, jnp.where(lane == 3, -hi, jnp.where(lane == 4, -mid,
                                                                             jnp.where(lane == 5, -lo, 0.0))))
    return tail.astype(BF16)


def _fox_prompt_kernel(q_ref, k_ref, v_ref, c_ref, buf_ref, o_ref, ka_scr, vt_scr, m_scr, l_scr, acc_scr, *,
                       hps, sub):
    del buf_ref
    hg = pl.program_id(1)
    qi = pl.program_id(2)
    tq = tk = q_ref.shape[0]
    seq = k_ref.shape[0]
    hd = q_ref.shape[1] // hps

    @pl.when(qi == 0)
    def _():
        eye =(lax.broadcasted_iota(jnp.int32, (hd, hd), 0)
               == lax.broadcasted_iota(jnp.int32, (hd, hd), 1)).astype(BF16)

        def build(r, carry):
            r0 = pl.multiple_of(r * tk, tk)
            cblk = c_ref[pl.ds(r0, tk), :] * LOG2E
            for j in range(hps):
                ka_scr[j, pl.ds(r0, tk), 0:hd] = k_ref[pl.ds(r0, tk), j * hd:(j + 1) * hd]
                ka_scr[j, pl.ds(r0, tk), hd:2 * hd] = _bias_tail(_select_lane(cblk, hg * hps + j), query=False)
                vt_scr[j, :, pl.ds(r0, tk)] = _dg(eye, v_ref[pl.ds(r0, tk), j * hd:(j + 1) * hd], NT).astype(BF16)
            return carry
        lax.fori_loop(0, seq // tk, build, 0)

    cq_blk = c_ref[pl.ds(pl.multiple_of(qi * tq, tq), tq), :] * LOG2E
    qa = [jnp.concatenate([q_ref[:, j * hd:(j + 1) * hd],
                           _bias_tail(_select_lane(cq_blk, hg * hps + j), query=True)], axis=1)
          for j in range(hps)]
    m_scr[...] = jnp.full_like(m_scr, -jnp.inf)
    l_scr[...] = jnp.zeros_like(l_scr)
    acc_scr[...] = jnp.zeros_like(acc_scr)

    def logits(j, ks):
        return _dg(ka_scr[j, pl.ds(ks, sub), :], qa[j], NT)

    def absorb(j, st, ks, diag_offset):
        if diag_offset is not None:
            key = lax.broadcasted_iota(jnp.int32, st.shape, 0)
            qry = lax.broadcasted_iota(jnp.int32, st.shape, 1)
            st = jnp.where(key + diag_offset <= qry, st, -jnp.inf)
        m_old = m_scr[j]
        m_new = jnp.maximum(m_old, jnp.max(st, axis=0, keepdims=True))
        pt = jnp.exp2(st - m_new)
        alpha = jnp.exp2(m_old - m_new)
        l_scr[j] = alpha * l_scr[j] + jnp.sum(pt, axis=0, keepdims=True)
        acc_scr[j] = alpha * acc_scr[j] + _dot(vt_scr[j, :, pl.ds(ks, sub)], pt.astype(BF16))
        m_scr[j] = m_new

    def key_block(k0, st_cur, masked):
        units = [(j, d) for d in range(tk // sub) for j in range(hps)]
        start = lambda d: pl.multiple_of(k0 + d * sub, sub)
        for n, (j, d) in enumerate(units):
            if n + 1 < len(units):
                st_next = logits(units[n + 1][0], start(units[n + 1][1]))
            else:
                st_next = None if masked else logits(units[0][0], pl.multiple_of(k0 + tk, sub))
            absorb(j, st_cur, start(d), d * sub if masked else None)
            st_cur = st_next
        return st_cur

    def below_diagonal(kk, st):
        return key_block(pl.multiple_of(kk * tk, tk), st, False)

    st_diag = lax.fori_loop(0, qi, below_diagonal, logits(0, 0))
    key_block(pl.multiple_of(qi * tk, tk), st_diag, True)
    for j in range(hps):
        o_ref[:, j * hd:(j + 1) * hd] = jnp.transpose(acc_scr[j] / l_scr[j]).astype(o_ref.dtype)


def _fox_prompt(qkv, c, out_buf, *, heads, n_seq, seq_len, tq, hps, sub):
    m, width = qkv.shape
    d = width // 3
    hd = d // heads
    nq = seq_len // tq
    ng = heads // hps
    return pl.pallas_call(
        functools.partial(_fox_prompt_kernel, hps=hps, sub=sub), grid=(n_seq, ng, nq),
        in_specs=[pl.BlockSpec((tq, hps * hd), lambda b, g, i: (b * nq + i, g)),
                  pl.BlockSpec((seq_len, hps * hd), lambda b, g, i: (b, ng + g), pipeline_mode=pl.Buffered(1)),
                  pl.BlockSpec((seq_len, hps * hd), lambda b, g, i: (b, 2 * ng + g), pipeline_mode=pl.Buffered(1)),
                  pl.BlockSpec((seq_len, 128), lambda b, g, i: (b, 0), pipeline_mode=pl.Buffered(1)),
                  pl.BlockSpec(memory_space=pl.ANY)],
        out_specs=pl.BlockSpec((tq, hps * hd), lambda b, g, i: (b * nq + i, g)),
        out_shape=jax.ShapeDtypeStruct((m, d), BF16), input_output_aliases={4: 0},
        scratch_shapes=[pltpu.VMEM((hps, seq_len, 2 * hd), BF16), pltpu.VMEM((hps, hd, seq_len), BF16),
                        pltpu.VMEM((hps, 1, tq), F32), pltpu.VMEM((hps, 1, tq), F32),
                        pltpu.VMEM((hps, hd, tq), F32)],
        compiler_params=_params("arbitrary", "arbitrary", "arbitrary"), name="fox_prompt")(
            qkv, qkv, qkv, c, out_buf)


def _fox_sample_kernel(q_ref, kn_ref, vn_ref, c_ref, kc_ref, vc_ref, lfc_ref, prev_ref, o_ref, d_scr, ct_scr):
    del prev_ref
    g = pl.program_id(1)
    past, hpg, hd = kc_ref.shape
    ts = q_ref.shape[0]

    @pl.when(g == 0)
    def _():
        later = (lax.broadcasted_iota(jnp.int32, (past, past), 0)
                 > lax.broadcasted_iota(jnp.int32, (past, past), 1)).astype(BF16)
        hi, mid, lo = _split3(lfc_ref[...])
        d_scr[...] = _dot(hi, later) + _dot(mid, later) + _dot(lo, later)
        ct_scr[...] = _transpose_exact(c_ref[...])

    c_blk = c_ref[...]
    row = lax.broadcasted_iota(jnp.int32, (ts, ts), 0)
    col = lax.broadcasted_iota(jnp.int32, (ts, ts), 1)
    for j in range(hpg):
        h = g * hpg + j
        cq = _select_lane(c_blk, h)
        d_row = d_scr[pl.ds(h, 1), :]
        cn_row = ct_scr[pl.ds(h, 1), :]
        sl = slice(j * hd, (j + 1) * hd)
        q = q_ref[:, sl]
        s_past = _dg(q, kc_ref[:, j, :].astype(BF16), NT) + (cq + d_row) * LOG2E
        s_new = _dg(q, kn_ref[:, sl], NT) + (cq - cn_row) * LOG2E
        s_new = jnp.where(col <= row, s_new, -jnp.inf)
        m = jnp.maximum(jnp.max(s_past, axis=1, keepdims=True), jnp.max(s_new, axis=1, keepdims=True))
        p_past = jnp.exp2(s_past - m)
        p_new = jnp.exp2(s_new - m)
        den = jnp.sum(p_past, axis=1, keepdims=True) + jnp.sum(p_new, axis=1, keepdims=True)
        o = _dot(p_past.astype(BF16), vc_ref[:, j, :].astype(BF16)) + _dot(p_new.astype(BF16), vn_ref[:, sl])
        o_ref[:, sl] = (o / den).astype(o_ref.dtype)


def _fox_sample(qkv, c, k_cache, v_cache, lf_cache_t, prev_out, *, heads, row0, ts, hpg=8):
    m, width = qkv.shape
    d = width // 3
    hd = d // heads
    n_streams, past = k_cache.shape[:2]
    base = row0 // ts
    ng = heads // hpg
    return pl.pallas_call(
        _fox_sample_kernel, grid=(n_streams, ng),
        in_specs=[pl.BlockSpec((ts, hpg * hd), lambda b, g: (base + b, g)),
                  pl.BlockSpec((ts, hpg * hd), lambda b, g: (base + b, ng + g)),
                  pl.BlockSpec((ts, hpg * hd), lambda b, g: (base + b, 2 * ng + g)),
                  pl.BlockSpec((ts, 128), lambda b, g: (base + b, 0)),
                  pl.BlockSpec((None, past, hpg, hd), lambda b, g: (b, 0, g, 0)),
                  pl.BlockSpec((None, past, hpg, hd), lambda b, g: (b, 0, g, 0)),
                  pl.BlockSpec((None, heads, past), lambda b, g: (b, 0, 0)),
                  pl.BlockSpec(memory_space=pl.ANY)],
        out_specs=pl.BlockSpec((ts, hpg * hd), lambda b, g: (base + b, g)),
        out_shape=jax.ShapeDtypeStruct((m, d), BF16),
        scratch_shapes=[pltpu.VMEM((heads, past), F32), pltpu.VMEM((128, ts), F32)],
        input_output_aliases={7: 0},
        compiler_params=_params("arbitrary", "arbitrary"), name="fox_sample")(
            qkv, qkv, qkv, c, k_cache, v_cache, lf_cache_t, prev_out)


def _pad_cols(w, n):
    return jnp.pad(w, ((0, 0), (0, n - w.shape[1])))


def _row_vec(v, n=None):
    v = v.astype(F32).reshape(1, -1)
    return v if n is None else _pad_cols(v, n)


def _block_diag(blocks, copies):
    g, p, _ = blocks.shape
    eye = jnp.eye(copies, dtype=blocks.dtype)
    return jnp.einsum("ab,gpq->gapbq", eye, blocks).reshape(g, copies * p, copies * p)


def _gmlp_spatial(w_s, b_s, block, group_dim):
    pos = jnp.arange(block)
    mask = (pos[None, :] // GMLP_CHUNK) <= (pos[:, None] // GMLP_CHUNK)
    ws = jnp.where(mask[None], w_s[:, :block, :block], 0.0)
    copies = ROW_TILE // block
    bias = jnp.repeat(jnp.tile(jnp.transpose(b_s[:, :block]), (copies, 1)), group_dim, axis=1)
    return _block_diag(ws, copies).astype(BF16), bias.astype(F32)


def _cumsum_matrix(block):
    r = jnp.arange(ROW_TILE)
    return ((r[None, :] <= r[:, None]) & (r[None, :] // block == r[:, None] // block)).astype(BF16)


def kernel(x_prompt, x_sample, state_b_C, state_b_n, state_b_m, state_c_S, cache_d_k, cache_d_v, cache_d_logf, a_w_in, a_b_in, a_vn_g, a_vn_b, a_w_s, a_b_s, a_w_out, b_w_in, b_b_gates, b_norm_g, b_w_out, c_w_in, c_gn_g, c_gn_b, c_w_out, d_w_in, d_b_f, d_w_out, ffn_w_in, ffn_w_out, ln1_g, ln1_b, ln2_g, ln2_b):
    nbp, seq, d = x_prompt.shape
    nbs, ts, _ = x_sample.shape
    mp, ms = nbp * seq, nbs * ts
    past = cache_d_k.shape[2]
    depth = ffn_w_in.shape[0]
    alpha = (2.0 * depth) ** 0.25
    assert ms == ROW_TILE == SEQ_CHUNK and ROW_TILE % ts == 0 and seq % ATTN_BLOCK == 0 and seq % SEQ_CHUNK == 0
    n_prompt_tiles = mp // ROW_TILE

    x = jnp.concatenate([x_prompt.reshape(mp, d), x_sample.reshape(ms, d)], axis=0)
    ffn_w_in_b, ffn_w_out_b = ffn_w_in.astype(BF16), ffn_w_out.astype(BF16)

    spent = None
    a_vs = []
    b_cp, b_np, b_mp, b_cs, b_ns, b_ms = [], [], [], [], [], []
    c_sp, c_ss = [], []
    d_kp, d_vp, d_fp, d_ks, d_vs, d_fs = [], [], [], [], [], []

    for i in range(depth):
        kind, j = i % 4, i // 4
        if kind == 0:
            w_in = a_w_in[j].astype(BF16)
            u, v, v_last = _gmlp_in(x, w_in[:, :d], _row_vec(a_b_in[j, :d]), w_in[:, d:], _row_vec(a_b_in[j, d:]),
                                    _row_vec(a_vn_g[j]), _row_vec(a_vn_b[j]))
            gd = d // GMLP_GROUPS
            wp, bp = _gmlp_spatial(a_w_s[j], a_b_s[j], GMLP_BLOCK, gd)
            wsm, bsm = _gmlp_spatial(a_w_s[j], a_b_s[j], ts, gd)
            mix = _gmlp_gate(u, v, jnp.stack([wp, wsm]), jnp.stack([bp, bsm]), n_prompt_tiles)
            w_out = a_w_out[j]
            a_vs.append(v_last[v_last.shape[0] - ms:].reshape(nbs, ts, d))
        elif kind == 1:
            heads = MLSTM_HEADS
            n_main = b_w_in.shape[2] - 2 * heads
            dqk = (n_main - 2 * d) // (2 * heads)
            qkv, gates = _qkv_proj(x, b_w_in[j, :, :n_main].astype(BF16),
                                   _pad_cols(b_w_in[j, :, n_main:], 128).astype(BF16), _row_vec(b_b_gates[j], 128),
                                   scaled_cols=heads * dqk, scale=dqk ** -0.5, n_raw=heads, tn=PROJ_WIDE_COLS)
            norm_g = _row_vec(b_norm_g[j])
            zc = jnp.zeros((nbp,) + state_b_C.shape[2:], F32)
            zn = jnp.zeros((nbp,) + state_b_n.shape[2:], F32)
            zm = jnp.zeros((nbp,) + state_b_m.shape[2:], F32)
            mix_s, cs, ns, ms_state = _mlstm(qkv, gates, norm_g, state_b_C[j], state_b_n[j], state_b_m[j],
                                             chunk=ts, row0=mp, seq_len=ts)
            mix, cp, np_, mp_state = _mlstm(qkv, gates, norm_g, zc, zn, zm, chunk=SEQ_CHUNK, row0=0, seq_len=seq,
                                            tail=mix_s)
            b_cp.append(cp); b_np.append(np_); b_mp.append(mp_state)
            b_cs.append(cs); b_ns.append(ns); b_ms.append(ms_state)
            w_out = b_w_out[j]
        elif kind == 2:
            heads = RET_HEADS
            dk = d // heads
            half = dk // 2
            inv = ROPE_BASE ** (-jnp.arange(half, dtype=F32) / half)
            ang_p = jnp.arange(seq, dtype=F32)[:, None] * inv[None, :]
            ang_s = (past + jnp.arange(ts)).astype(F32)[:, None] * inv[None, :]
            rows = lambda f: jnp.concatenate([jnp.tile(f(ang_p), (nbp, 1)), jnp.tile(f(ang_s), (nbs, 1))])
            qkvg = _rot_proj(x, c_w_in[j].astype(BF16), rows(jnp.cos), rows(jnp.sin),
                             n_q=d // PROJ_WIDE_COLS, n_k=d // PROJ_WIDE_COLS, k_scale=dk ** -0.5, head=dk,
                             tn=PROJ_WIDE_COLS)
            gn_g, gn_b = _row_vec(c_gn_g[j]), _row_vec(c_gn_b[j])
            zs = jnp.zeros((nbp,) + state_c_S.shape[2:], F32)
            mix_s, ss = _retention(qkvg, gn_g, gn_b, state_c_S[j], chunk=ts, row0=mp, seq_len=ts)
            mix, sp = _retention(qkvg, gn_g, gn_b, zs, chunk=SEQ_CHUNK, row0=0, seq_len=seq, tail=mix_s)
            c_sp.append(sp); c_ss.append(ss)
            w_out = c_w_out[j]
        else:
            heads = FOX_HEADS
            hd = d // heads
            qkv, logf, k_p, k_s, v_p, v_s = _qkv_proj(
                x, d_w_in[j, :, :3 * d].astype(BF16), _pad_cols(d_w_in[j, :, 3 * d:], 128).astype(BF16),
                _row_vec(d_b_f[j], 128), scaled_cols=d, scale=hd ** -0.5 * LOG2E, n_raw=0, kv_rows=(mp, ms))
            tri = jnp.stack([_cumsum_matrix(ROW_TILE), _cumsum_matrix(ts)])
            c = _running_sum(logf, tri, seq // ROW_TILE, n_prompt_tiles)
            out_buf = spent if spent is not None else jnp.zeros((mp + ms, d), BF16)
            mix = _fox_prompt(qkv, c, out_buf, heads=heads, n_seq=nbp, seq_len=seq, tq=ATTN_BLOCK,
                              hps=ATTN_HEADS_PER_STEP, sub=ATTN_KEY_SUB)
            spent = None
            mix = _fox_sample(qkv, c, cache_d_k[j], cache_d_v[j],
                              jnp.transpose(cache_d_logf[j], (0, 2, 1)), mix, heads=heads, row0=mp, ts=ts)
            d_kp.append(k_p.reshape(nbp, seq, heads, hd)); d_ks.append(k_s.reshape(nbs, ts, heads, hd))
            d_vp.append(v_p.reshape(nbp, seq, heads, hd)); d_vs.append(v_s.reshape(nbs, ts, heads, hd))
            d_fp.append(logf[:mp, :heads].reshape(nbp, seq, heads))
            d_fs.append(logf[mp:, :heads].reshape(nbs, ts, heads))
            w_out = d_w_out[j]
        x = _out_ln(mix, w_out.astype(BF16), x, _row_vec(ln1_g[i]), _row_vec(ln1_b[i]), alpha)
        if mix.shape == (mp + ms, d):
            spent = mix
        x = _ffn(x, ffn_w_in_b, ffn_w_out_b, i, _row_vec(ln2_g[i]), _row_vec(ln2_b[i]), alpha,
                 final_rows=(mp, ms) if i == depth - 1 else None)

    y_prompt, y_sample = x
    return (y_prompt.reshape(nbp, seq, d), y_sample.reshape(nbs, ts, d), jnp.stack(a_vs),
            jnp.stack(b_cp), jnp.stack(b_np), jnp.stack(b_mp),
            jnp.stack(b_cs), jnp.stack(b_ns), jnp.stack(b_ms),
            jnp.stack(c_sp), jnp.stack(c_ss),
            jnp.stack(d_kp), jnp.stack(d_vp), jnp.stack(d_fp),
            jnp.stack(d_ks), jnp.stack(d_vs), jnp.stack(d_fs))
```

```python
import functools
import math

import jax
import jax.numpy as jnp
from jax import lax
from jax.experimental import pallas as pl
from jax.experimental.pallas import tpu as pltpu

F32 = jnp.float32
BF16 = jnp.bfloat16

LN_EPS = 1e-5
ROPE_BASE = 10000.0
GMLP_BLOCK = 128
GMLP_CHUNK = 64
GMLP_GROUPS = 8
MLSTM_HEADS = 8
RET_HEADS = 8
FOX_HEADS = 16
LANES = 128
ROW_TILE = 256
SUM_TILE = 1024
SEQ_CHUNK = 256
ATTN_BLOCK = 512
ATTN_HEADS_PER_STEP = 4
PROJ_WIDE_COLS = 2048
ATTN_KEY_SUB = 256
LOG2E = math.log2(math.e)
VMEM_LIMIT = 56 * 2 ** 20

NT = (((1,), (1,)), ((), ()))
TN = (((0,), (0,)), ((), ()))


def _params(*sem):
    return pltpu.CompilerParams(dimension_semantics=sem, vmem_limit_bytes=VMEM_LIMIT)


def _row_tile(m, candidates):
    for t in candidates:
        if m % t == 0:
            return t
    raise ValueError(f"no row tile for {m}")


def _dot(a, b):
    return jnp.dot(a, b, preferred_element_type=F32)


def _dg(a, b, dims):
    return lax.dot_general(a, b, dims, preferred_element_type=F32)


def _split3(x):
    hi = x.astype(BF16)
    r1 = x - hi.astype(F32)
    mid = r1.astype(BF16)
    lo = (r1 - mid.astype(F32)).astype(BF16)
    return hi, mid, lo


def _dot_exact_lhs(mat_bf16, x):
    hi, mid, lo = _split3(x)
    return _dot(mat_bf16, hi) + _dot(mat_bf16, mid) + _dot(mat_bf16, lo)


def _transpose_exact(x):
    n = x.shape[1]
    eye = (lax.broadcasted_iota(jnp.int32, (n, n), 0) == lax.broadcasted_iota(jnp.int32, (n, n), 1)).astype(BF16)
    hi, mid, lo = _split3(x)
    return _dg(eye, hi, NT) + _dg(eye, mid, NT) + _dg(eye, lo, NT)


def _sigmoid(x):
    return 1.0 / (1.0 + jnp.exp(-x))


def _log_sigmoid(x):
    return jnp.minimum(x, 0.0) - jnp.log1p(jnp.exp(-jnp.abs(x)))


def _gelu_tanh(x):
    return 0.5 * x * (1.0 + jnp.tanh(math.sqrt(2.0 / math.pi) * (x + 0.044715 * (x * x * x))))


def _layer_norm(y, g, b):
    mu = jnp.mean(y, axis=-1, keepdims=True)
    d = y - mu
    var = jnp.mean(d * d, axis=-1, keepdims=True)
    return d * lax.rsqrt(var + LN_EPS) * g + b


def _matmul_row_parts(lhs_ref, w_ref, epilogue, cast=False):
    tm = lhs_ref.shape[0]
    n_parts = 3 if tm % 48 == 0 else 2
    part = tm // n_parts
    rows = [slice(p * part, (p + 1) * part) for p in range(n_parts)]

    def matmul(r):
        lhs = lhs_ref[r, :]
        return _dot(lhs.astype(BF16) if cast else lhs, w_ref[...])

    acc = matmul(rows[0])
    for p, r in enumerate(rows):
        nxt = matmul(rows[p + 1]) if p + 1 < n_parts else None
        epilogue(r, acc)
        acc = nxt


def _cast_rows_once(x_ref, xb_scr, j):
    @pl.when(j == 0)
    def _():
        xb_scr[...] = x_ref[...].astype(BF16)


def _gmlp_in_kernel(x_ref, wu_ref, bu_ref, wv_ref, bv_ref, g_ref, be_ref, u_ref, v_ref, last_ref):
    def u_epilogue(r, acc):
        u_ref[r, :] = _gelu_tanh(acc + bu_ref[...]).astype(u_ref.dtype)

    def v_epilogue(r, acc):
        v = _layer_norm(_gelu_tanh(acc + bv_ref[...]), g_ref[...], be_ref[...])
        v_ref[r, :] = v.astype(v_ref.dtype)
        last_ref[r, :] = v

    _matmul_row_parts(x_ref, wu_ref, u_epilogue, cast=True)
    _matmul_row_parts(x_ref, wv_ref, v_epilogue, cast=True)


def _gmlp_in(x, w_u, b_u, w_v, b_v, vn_g, vn_b):
    m, d = x.shape
    tm = _row_tile(m, (384, 256))
    n = w_u.shape[1]
    row = pl.BlockSpec((tm, d), lambda i: (i, 0))
    wsp = pl.BlockSpec((d, n), lambda i: (0, 0), pipeline_mode=pl.Buffered(1))
    vec = pl.BlockSpec((1, n), lambda i: (0, 0))
    out = pl.BlockSpec((tm, n), lambda i: (i, 0))
    return pl.pallas_call(
        _gmlp_in_kernel, grid=(m // tm,), in_specs=[row, wsp, vec, wsp, vec, vec, vec],
        out_specs=[out, out, pl.BlockSpec((tm, n), lambda i: (0, 0))],
        out_shape=[jax.ShapeDtypeStruct((m, n), BF16), jax.ShapeDtypeStruct((m, n), BF16),
                   jax.ShapeDtypeStruct((tm, n), F32)],
        compiler_params=_params("arbitrary"), name="gmlp_in")(x, w_u, b_u, w_v, b_v, vn_g, vn_b)


def _qkv_proj_kernel(x_ref, w_ref, wg_ref, bg_ref, o_ref, gate_ref, *refs, scaled_cols, scale, nb, tail, n_raw):
    kv_refs, xb_scr = refs[:-1], refs[-1]
    j = pl.program_id(1)
    _cast_rows_once(x_ref, xb_scr, j)

    @pl.when(j == 0)
    def _():
        g = _dot(xb_scr[...], wg_ref[...]) + bg_ref[...]
        lane = lax.broadcasted_iota(jnp.int32, g.shape, 1)
        gate_ref[...] = jnp.where(lane < n_raw, g, _log_sigmoid(g))

    tn = o_ref.shape[1]
    col = j * tn + lax.broadcasted_iota(jnp.int32, (1, tn), 1)
    col_scale = jnp.where(col < scaled_cols, scale, 1.0)
    accs = []

    def epilogue(r, acc):
        o_ref[r, :] = (acc * col_scale).astype(o_ref.dtype)
        accs.append(acc)

    _matmul_row_parts(xb_scr, w_ref, epilogue)

    if kv_refs:
        kp_ref, ks_ref, vp_ref, vs_ref = kv_refs
        tm, tn = o_ref.shape
        acc = jnp.concatenate(accs, axis=0)

        @pl.when(jnp.logical_and(j >= nb, j < 2 * nb))
        def _():
            kp_ref[...] = acc
            ks_ref[:, pl.ds(pl.multiple_of((j - nb) * tn, tn), tn)] = acc[tm - tail:]

        @pl.when(j >= 2 * nb)
        def _():
            vp_ref[...] = acc
            vs_ref[:, pl.ds(pl.multiple_of((j - 2 * nb) * tn, tn), tn)] = acc[tm - tail:]


def _qkv_proj(x, w, w_gate, b_gate, *, scaled_cols, scale, n_raw, kv_rows=None, tn=1024):
    m, d = x.shape
    n = w.shape[1]
    tm = _row_tile(m, (768, 512, 256))
    in_specs = [pl.BlockSpec((tm, d), lambda i, j: (i, 0)), pl.BlockSpec((d, tn), lambda i, j: (0, j)),
                pl.BlockSpec((d, 128), lambda i, j: (0, 0)), pl.BlockSpec((1, 128), lambda i, j: (0, 0))]
    out_specs = [pl.BlockSpec((tm, tn), lambda i, j: (i, j)), pl.BlockSpec((tm, 128), lambda i, j: (i, 0))]
    out_shape = [jax.ShapeDtypeStruct((m, n), BF16), jax.ShapeDtypeStruct((m, 128), F32)]
    nb, tail = n // (3 * tn), 0
    if kv_rows is not None:
        mp, tail = kv_rows
        assert mp + tail == m and tail <= tm
        for first in (nb, 2 * nb):
            col = functools.partial(lambda j, first: jnp.clip(j - first, 0, nb - 1), first=first)
            out_specs += [pl.BlockSpec((tm, tn), lambda i, j, col=col: (i, col(j))),
                          pl.BlockSpec((tail, n // 3), lambda i, j: (0, 0))]
            out_shape += [jax.ShapeDtypeStruct((mp, n // 3), F32), jax.ShapeDtypeStruct((tail, n // 3), F32)]
    return pl.pallas_call(
        functools.partial(_qkv_proj_kernel, scaled_cols=scaled_cols, scale=scale, nb=nb, tail=tail, n_raw=n_raw),
        grid=(m // tm, n // tn), in_specs=in_specs, out_specs=out_specs, out_shape=out_shape,
        scratch_shapes=[pltpu.VMEM((tm, d), BF16)],
        compiler_params=_params("arbitrary", "arbitrary"), name="qkv_proj")(x, w, w_gate, b_gate)


def _rot_proj_kernel(x_ref, w_ref, cos_ref, sin_ref, o_ref, xb_scr, *, n_q, n_k, k_scale, head):
    j = pl.program_id(1)
    _cast_rows_once(x_ref, xb_scr, j)
    half = head // 2
    rotated = j < n_q + n_k
    col_scale = jnp.where(jnp.logical_and(j >= n_q, rotated), k_scale, 1.0).astype(F32)

    def epilogue(r, acc):
        cos = jnp.where(rotated, cos_ref[r, :], 1.0)
        sin = jnp.where(rotated, sin_ref[r, :], 0.0)
        for h in range(acc.shape[1] // head):
            x1 = acc[:, h * head:h * head + half]
            x2 = acc[:, h * head + half:(h + 1) * head]
            o_ref[r, h * head:h * head + half] = ((x1 * cos - x2 * sin) * col_scale).astype(o_ref.dtype)
            o_ref[r, h * head + half:(h + 1) * head] = ((x1 * sin + x2 * cos) * col_scale).astype(o_ref.dtype)

    _matmul_row_parts(xb_scr, w_ref, epilogue)


def _rot_proj(x, w, cos, sin, *, n_q, n_k, k_scale, head, tn=1024):
    m, d = x.shape
    n = w.shape[1]
    tm = _row_tile(m, (768, 512, 256))
    half = head // 2
    return pl.pallas_call(
        functools.partial(_rot_proj_kernel, n_q=n_q, n_k=n_k, k_scale=k_scale, head=head),
        grid=(m // tm, n // tn),
        in_specs=[pl.BlockSpec((tm, d), lambda i, j: (i, 0)), pl.BlockSpec((d, tn), lambda i, j: (0, j)),
                  pl.BlockSpec((tm, half), lambda i, j: (i, 0)), pl.BlockSpec((tm, half), lambda i, j: (i, 0))],
        out_specs=pl.BlockSpec((tm, tn), lambda i, j: (i, j)),
        out_shape=jax.ShapeDtypeStruct((m, n), BF16), scratch_shapes=[pltpu.VMEM((tm, d), BF16)],
        compiler_params=_params("arbitrary", "arbitrary"), name="rot_proj")(x, w, cos, sin)


def _running_sum_kernel(lf_ref, tri_ref, c_ref, carry_ref, *, tiles_per_seq):
    i = pl.program_id(0)

    @pl.when(i % tiles_per_seq == 0)
    def _():
        carry_ref[...] = jnp.zeros_like(carry_ref)

    c = _dot_exact_lhs(tri_ref[...], lf_ref[...]) + carry_ref[...]
    c_ref[...] = c
    carry_ref[...] = c[-1:, :]


def _running_sum(lf, tri, *, row0, n_tiles, tiles_per_seq):
    tile = tri.shape[0]
    return pl.pallas_call(
        functools.partial(_running_sum_kernel, tiles_per_seq=tiles_per_seq), grid=(n_tiles,),
        in_specs=[pl.BlockSpec((tile, 128), lambda i: (row0 // tile + i, 0)),
                  pl.BlockSpec((tile, tile), lambda i: (0, 0))],
        out_specs=pl.BlockSpec((tile, 128), lambda i: (i, 0)),
        out_shape=jax.ShapeDtypeStruct((n_tiles * tile, 128), F32),
        scratch_shapes=[pltpu.VMEM((1, 128), F32)],
        compiler_params=_params("arbitrary"), name="running_sum")(lf, tri)


def _out_ln_kernel(h_ref, w_ref, x_ref, g_ref, b_ref, o_ref, *, alpha):
    def epilogue(r, acc):
        o_ref[r, :] = _layer_norm(alpha * x_ref[r, :] + acc, g_ref[...], b_ref[...])

    _matmul_row_parts(h_ref, w_ref, epilogue)


def _out_ln(h, w, x, g, b, alpha):
    m, k = h.shape
    d = w.shape[1]
    tm = _row_tile(m, (768, 384, 256) if k <= d else (384, 256))
    vec = pl.BlockSpec((1, d), lambda i: (0, 0))
    row = pl.BlockSpec((tm, d), lambda i: (i, 0))
    return pl.pallas_call(
        functools.partial(_out_ln_kernel, alpha=alpha), grid=(m // tm,),
        in_specs=[pl.BlockSpec((tm, k), lambda i: (i, 0)),
                  pl.BlockSpec((k, d), lambda i: (0, 0), pipeline_mode=pl.Buffered(1)),
                  row, vec, vec],
        out_specs=row, out_shape=jax.ShapeDtypeStruct((m, d), F32),
        compiler_params=_params("arbitrary"), name="out_ln")(h, w, x, g, b)


def _ffn_kernel(x_ref, wg_ref, wu_ref, wo_ref, g_ref, b_ref, o_ref, *rest, alpha):
    tail_refs, xb_scr = rest[:-1], rest[-1]
    j = pl.program_id(1)
    _cast_rows_once(x_ref, xb_scr, j)

    @pl.when(j == 0)
    def _():
        o_ref[...] = jnp.zeros_like(o_ref)

    xb = xb_scr[...]
    gate = _dot(xb, wg_ref[...])
    up = _dot(xb, wu_ref[...])
    hid = (gate * _sigmoid(gate) * up).astype(BF16)
    o_ref[...] += _dot(hid, wo_ref[...])

    @pl.when(j == pl.num_programs(1) - 1)
    def _():
        y = _layer_norm(alpha * x_ref[...] + o_ref[...], g_ref[...], b_ref[...])
        o_ref[...] = y
        for t_ref in tail_refs:
            t_ref[...] = y[y.shape[0] - t_ref.shape[0]:]


def _ffn(x, w_in, w_out, layer, g, b, alpha, final_rows=None, th=512):
    m, d = x.shape
    hidden = w_out.shape[1]
    tm = _row_tile(m, (768, 512, 256))
    nh = hidden // th
    vec = pl.BlockSpec((1, d), lambda i, j: (0, 0))
    row = pl.BlockSpec((tm, d), lambda i, j: (i, 0))
    if final_rows is None:
        out_specs, out_shape = row, jax.ShapeDtypeStruct((m, d), F32)
    else:
        mp, tail = final_rows
        assert mp + tail == m and tail <= tm
        out_specs = [row, pl.BlockSpec((tail, d), lambda i, j: (0, 0))]
        out_shape = [jax.ShapeDtypeStruct((mp, d), F32), jax.ShapeDtypeStruct((tail, d), F32)]
    return pl.pallas_call(
        functools.partial(_ffn_kernel, alpha=alpha), grid=(m // tm, nh),
        in_specs=[row, pl.BlockSpec((None, d, th), lambda i, j: (layer, 0, j)),
                  pl.BlockSpec((None, d, th), lambda i, j: (layer, 0, j + nh)),
                  pl.BlockSpec((None, th, d), lambda i, j: (layer, j, 0)), vec, vec],
        out_specs=out_specs, out_shape=out_shape,
        scratch_shapes=[pltpu.VMEM((tm, d), BF16)],
        compiler_params=_params("arbitrary", "arbitrary"), name="ffn")(x, w_in, w_in, w_out, g, b)


def _gmlp_gate_kernel(u_ref, v_ref, w_ref, b_ref, y_ref, *, groups):
    gd = u_ref.shape[1] // groups
    for g in range(groups):
        sl = slice(g * gd, (g + 1) * gd)
        s = _dot(w_ref[g], v_ref[:, sl]) + b_ref[:, sl]
        y_ref[:, sl] = (u_ref[:, sl].astype(F32) * s).astype(y_ref.dtype)


def _gmlp_gate(u, v, w_sp, b_sp, n_prompt_tiles):
    m, d = u.shape
    tm = ROW_TILE
    groups = w_sp.shape[1]
    row = pl.BlockSpec((tm, d), lambda i: (i, 0))
    return pl.pallas_call(
        functools.partial(_gmlp_gate_kernel, groups=groups), grid=(m // tm,),
        in_specs=[row, row,
                  pl.BlockSpec((None, groups, tm, tm), lambda i: (i // n_prompt_tiles, 0, 0, 0)),
                  pl.BlockSpec((None, tm, d), lambda i: (i // n_prompt_tiles, 0, 0))],
        out_specs=row, out_shape=jax.ShapeDtypeStruct((m, d), BF16),
        compiler_params=_params("arbitrary"), name="gmlp_gate")(u, v, w_sp, b_sp)


def _chunked_mixer_kernel(step_fn, n_in, n_out, n_chunks, n_main, *refs):
    ins, outs = refs[:n_in], refs[len(refs) - n_out:]
    s = pl.program_id(0)
    if len(refs) > n_in + n_out:
        @pl.when(s == n_main)
        def _():
            outs[0][...] = refs[n_in][...]

    @pl.when(s < n_main)
    def _():
        step_fn(*ins, *outs, first=(s % n_chunks == 0))


def _mlstm_step(qkv_ref, gt_ref, ng_ref, c0_ref, n0_ref, m0_ref, h_ref, c_ref, n_ref, m_ref, *, first, heads):
    d = qkv_ref.shape[1] // 3
    q_ref, k_ref = qkv_ref.at[:, 0:d // 2], qkv_ref.at[:, d // 2:d]
    v_ref, og_ref = qkv_ref.at[:, d:2 * d], qkv_ref.at[:, 2 * d:3 * d]
    L = q_ref.shape[0]
    dqk = q_ref.shape[1] // heads
    dv = v_ref.shape[1] // heads

    @pl.when(first)
    def _():
        c_ref[...] = c0_ref[...]
        n_ref[...] = n0_ref[...]
        m_ref[...] = m0_ref[...]

    g = gt_ref[...]
    row = lax.broadcasted_iota(jnp.int32, (L, L), 0)
    col = lax.broadcasted_iota(jnp.int32, (L, L), 1)
    causal = col <= row
    bc = _dot_exact_lhs(causal.astype(BF16), g)
    g_t = _transpose_exact(g)
    bc_t = _transpose_exact(bc)

    ones_cols = jnp.ones((L, LANES), BF16)
    mean_cols = jnp.full((dv, LANES), 1.0 / dv, BF16)

    def rep(col):
        return jnp.broadcast_to(col, (L, LANES))

    def wide(x_rep, cols):
        return x_rep[:, :cols] if cols < LANES else jnp.concatenate([x_rep] * (cols // LANES), axis=1)

    for h in range(heads):
        a = rep(bc[:, heads + h:heads + h + 1])
        li = rep(g[:, h:h + 1])
        a_row = bc_t[heads + h:heads + h + 1, :]
        li_row = g_t[h:h + 1, :]
        m_prev = m_ref[h:h + 1, :]
        dmat = jnp.where(causal, wide(a, L) - a_row + li_row, -jnp.inf)
        inter = a + m_prev
        m_t = jnp.maximum(inter, rep(jnp.max(dmat, axis=1, keepdims=True)))
        w_intra = jnp.exp(dmat - wide(m_t, L))
        w_inter = jnp.exp(inter - m_t)
        qh = q_ref[:, h * dqk:(h + 1) * dqk]
        kh = k_ref[:, h * dqk:(h + 1) * dqk]
        vh = v_ref[:, h * dv:(h + 1) * dv]
        c_old = c_ref[h]
        n_old = n_ref[h:h + 1, :]
        scores = (_dg(qh, kh, NT) * w_intra).astype(BF16)
        v_ones = jnp.concatenate([vh, ones_cols], axis=1)
        c_n = jnp.concatenate([c_old, jnp.broadcast_to(n_old, (LANES, dqk))], axis=0).astype(BF16)
        both = _dot(scores, v_ones) + wide(w_inter, dv + LANES) * _dg(qh, c_n, NT)
        inv = 1.0 / jnp.maximum(jnp.abs(both[:, dv:]), jnp.exp(-m_t))
        hid = both[:, :dv] * wide(inv, dv)
        mean_sq = _dot((hid * hid).astype(BF16), mean_cols)
        sl = slice(h * dv, (h + 1) * dv)
        y = hid * wide(lax.rsqrt(mean_sq + LN_EPS), dv) * ng_ref[:, sl]
        h_ref[:, sl] = (y * _sigmoid(og_ref[:, sl].astype(F32))).astype(h_ref.dtype)

        m_new = m_t[L - 1:L, :]
        b_last = a[L - 1:L, :]
        w_c = jnp.exp(b_last + m_prev - m_new)
        w_s = jnp.exp(b_last - a + li - m_new)
        c_ref[h] = w_c * c_old + _dg((vh.astype(F32) * wide(w_s, dv)).astype(BF16), kh, TN)
        n_ref[h:h + 1, :] = w_c * n_old + jnp.sum(kh.astype(F32) * w_s, axis=0, keepdims=True)
        m_ref[h:h + 1, :] = m_new[:, 0:1]


def _mlstm(qkv, gates, norm_g, c0, n0, m0, *, chunk, row0, seq_len, tail=None):
    width = qkv.shape[1]
    d = width // 3
    nb, heads, dv, dqk = c0.shape
    nc = seq_len // chunk
    base = row0 // chunk
    n_main = nb * nc

    def row(s):
        return base + jnp.minimum(s, n_main - 1)

    def seq(s):
        return jnp.minimum(s // nc, nb - 1)

    state_specs = [pl.BlockSpec((None, heads, dv, dqk), lambda s: (seq(s), 0, 0, 0)),
                   pl.BlockSpec((None, heads, dqk), lambda s: (seq(s), 0, 0)),
                   pl.BlockSpec((None, heads, 1), lambda s: (seq(s), 0, 0))]
    in_specs = [pl.BlockSpec((chunk, width), lambda s: (row(s), 0)),
                pl.BlockSpec((chunk, 128), lambda s: (row(s), 0)),
                pl.BlockSpec((1, d), lambda s: (0, 0))] + state_specs
    args = [qkv, gates, norm_g, c0, n0, m0.reshape(nb, heads, 1)]
    if tail is not None:
        assert tail.shape == (chunk, d)
        in_specs.append(pl.BlockSpec((chunk, d), lambda s: (0, 0)))
        args.append(tail)
    n_steps = n_main + (tail is not None)
    outs = pl.pallas_call(
        functools.partial(_chunked_mixer_kernel, functools.partial(_mlstm_step, heads=heads), 6, 4, nc, n_main),
        grid=(n_steps,), in_specs=in_specs,
        out_specs=[pl.BlockSpec((chunk, d), lambda s: (s, 0))] + state_specs,
        out_shape=[jax.ShapeDtypeStruct((n_steps * chunk, d), BF16), jax.ShapeDtypeStruct(c0.shape, F32),
                   jax.ShapeDtypeStruct(n0.shape, F32), jax.ShapeDtypeStruct((nb, heads, 1), F32)],
        compiler_params=_params("arbitrary"), name="mlstm")(*args)
    return outs[0], outs[1], outs[2], outs[3].reshape(nb, heads)


def _retention_step(qkvg_ref, gg_ref, gb_ref, s0_ref, o_ref, s_ref, *, first, heads):
    d = qkvg_ref.shape[1] // 6
    q_ref, k_ref = qkvg_ref.at[:, 0:d], qkvg_ref.at[:, d:2 * d]
    v_ref, g_ref = qkvg_ref.at[:, 2 * d:4 * d], qkvg_ref.at[:, 4 * d:6 * d]
    L = q_ref.shape[0]
    dk = q_ref.shape[1] // heads
    dv = v_ref.shape[1] // heads

    @pl.when(first)
    def _():
        s_ref[...] = s0_ref[...]

    row = lax.broadcasted_iota(jnp.int32, (L, L), 0)
    col = lax.broadcasted_iota(jnp.int32, (L, L), 1)
    causal = col <= row
    lag = (row - col).astype(F32)
    t = lax.broadcasted_iota(jnp.int32, (L, 1), 0).astype(F32)

    for h in range(heads):
        log_gamma = math.log1p(-(2.0 ** (-5.0 - h)))
        decay = jnp.where(causal, jnp.exp(lag * log_gamma), 0.0)
        decay_q = jnp.exp((t + 1.0) * log_gamma)
        decay_k = jnp.exp((L - 1.0 - t) * log_gamma)
        decay_s = math.exp(L * log_gamma)
        qh = q_ref[:, h * dk:(h + 1) * dk]
        kh = k_ref[:, h * dk:(h + 1) * dk]
        vh = v_ref[:, h * dv:(h + 1) * dv]
        s_old = s_ref[h]
        scores = _dg(qh, kh, NT) * decay
        o = _dot(scores.astype(BF16), vh) + _dot(qh, s_old.astype(BF16)) * decay_q
        s_ref[h] = decay_s * s_old + _dg((kh.astype(F32) * decay_k).astype(BF16), vh, TN)
        sl = slice(h * dv, (h + 1) * dv)
        y = _layer_norm(o, gg_ref[:, sl], gb_ref[:, sl])
        gate = g_ref[:, sl].astype(F32)
        o_ref[:, sl] = (gate * _sigmoid(gate) * y).astype(o_ref.dtype)


def _retention(qkvg, gn_g, gn_b, s0, *, chunk, row0, seq_len, tail=None):
    width = qkvg.shape[1]
    d = width // 6
    nb, heads, dk, dv = s0.shape
    nc = seq_len // chunk
    base = row0 // chunk
    n_main = nb * nc
    state_spec = pl.BlockSpec((None, heads, dk, dv), lambda s: (jnp.minimum(s // nc, nb - 1), 0, 0, 0))
    in_specs = [pl.BlockSpec((chunk, width), lambda s: (base + jnp.minimum(s, n_main - 1), 0)),
                pl.BlockSpec((1, 2 * d), lambda s: (0, 0)),
                pl.BlockSpec((1, 2 * d), lambda s: (0, 0)), state_spec]
    args = [qkvg, gn_g, gn_b, s0]
    if tail is not None:
        assert tail.shape == (chunk, 2 * d)
        in_specs.append(pl.BlockSpec((chunk, 2 * d), lambda s: (0, 0)))
        args.append(tail)
    n_steps = n_main + (tail is not None)
    return pl.pallas_call(
        functools.partial(_chunked_mixer_kernel, functools.partial(_retention_step, heads=heads), 4, 2, nc, n_main),
        grid=(n_steps,), in_specs=in_specs,
        out_specs=[pl.BlockSpec((chunk, 2 * d), lambda s: (s, 0)), state_spec],
        out_shape=[jax.ShapeDtypeStruct((n_steps * chunk, 2 * d), BF16), jax.ShapeDtypeStruct(s0.shape, F32)],
        compiler_params=_params("arbitrary"), name="retention")(*args)


def _select_lane(block, lane_index):
    lane = lax.broadcasted_iota(jnp.int32, block.shape, 1)
    return jnp.sum(jnp.where(lane == lane_index, block, 0.0), axis=1, keepdims=True)


def _bias_tail(c_col, *, query):
    hi, mid, lo = (t.astype(F32) for t in _split3(c_col))
    lane = lax.broadcasted_iota(jnp.int32, (c_col.shape[0], 128), 1)
    if query:
        tail = jnp.where(lane == 0, hi, jnp.where(lane == 1, mid, jnp.where(lane == 2, lo,
                                                                             jnp.where(lane < 6, 1.0, 0.0))))
    else:
        tail = jnp.where(lane < 3, 1.0, jnp.where(lane == 3, -hi, jnp.where(lane == 4, -mid,
                                                                             jnp.where(lane == 5, -lo, 0.0))))
    return tail.astype(BF16)


def _fox_prompt_kernel(q_ref, k_ref, v_ref, c_ref, buf_ref, o_ref, ka_scr, vt_scr, m_scr, l_scr, acc_scr, *,
                       hps, sub):
    del buf_ref
    hg = pl.program_id(1)
    qi = pl.program_id(2)
    tq = tk = q_ref.shape[0]
    seq = k_ref.shape[0]
    hd = q_ref.shape[1] // hps

    @pl.when(qi == 0)
    def _():
        eye =(lax.broadcasted_iota(jnp.int32, (hd, hd), 0)
               == lax.broadcasted_iota(jnp.int32, (hd, hd), 1)).astype(BF16)

        def build(r, carry):
            r0 = pl.multiple_of(r * tk, tk)
            cblk = c_ref[pl.ds(r0, tk), :] * LOG2E
            for j in range(hps):
                ka_scr[j, pl.ds(r0, tk), 0:hd] = k_ref[pl.ds(r0, tk), j * hd:(j + 1) * hd]
                ka_scr[j, pl.ds(r0, tk), hd:2 * hd] = _bias_tail(_select_lane(cblk, hg * hps + j), query=False)
                vt_scr[j, :, pl.ds(r0, tk)] = _dg(eye, v_ref[pl.ds(r0, tk), j * hd:(j + 1) * hd], NT).astype(BF16)
            return carry
        lax.fori_loop(0, seq // tk, build, 0)

    cq_blk = c_ref[pl.ds(pl.multiple_of(qi * tq, tq), tq), :] * LOG2E
    qa = [jnp.concatenate([q_ref[:, j * hd:(j + 1) * hd],
                           _bias_tail(_select_lane(cq_blk, hg * hps + j), query=True)], axis=1)
          for j in range(hps)]
    m_scr[...] = jnp.full_like(m_scr, -jnp.inf)
    l_scr[...] = jnp.zeros_like(l_scr)
    acc_scr[...] = jnp.zeros_like(acc_scr)

    def logits(j, ks):
        return _dg(ka_scr[j, pl.ds(ks, sub), :], qa[j], NT)

    def absorb(j, st, ks, diag_offset):
        if diag_offset is not None:
            key = lax.broadcasted_iota(jnp.int32, st.shape, 0)
            qry = lax.broadcasted_iota(jnp.int32, st.shape, 1)
            st = jnp.where(key + diag_offset <= qry, st, -jnp.inf)
        m_old = m_scr[j]
        m_new = jnp.maximum(m_old, jnp.max(st, axis=0, keepdims=True))
        pt = jnp.exp2(st - m_new)
        alpha = jnp.exp2(m_old - m_new)
        l_scr[j] = alpha * l_scr[j] + jnp.sum(pt, axis=0, keepdims=True)
        acc_scr[j] = alpha * acc_scr[j] + _dot(vt_scr[j, :, pl.ds(ks, sub)], pt.astype(BF16))
        m_scr[j] = m_new

    def key_block(k0, st_cur, masked):
        units = [(j, d) for d in range(tk // sub) for j in range(hps)]
        start = lambda d: pl.multiple_of(k0 + d * sub, sub)
        for n, (j, d) in enumerate(units):
            if n + 1 < len(units):
                st_next = logits(units[n + 1][0], start(units[n + 1][1]))
            else:
                st_next = None if masked else logits(units[0][0], pl.multiple_of(k0 + tk, sub))
            absorb(j, st_cur, start(d), d * sub if masked else None)
            st_cur = st_next
        return st_cur

    def below_diagonal(kk, st):
        return key_block(pl.multiple_of(kk * tk, tk), st, False)

    st_diag = lax.fori_loop(0, qi, below_diagonal, logits(0, 0))
    key_block(pl.multiple_of(qi * tk, tk), st_diag, True)
    for j in range(hps):
        o_ref[:, j * hd:(j + 1) * hd] = jnp.transpose(acc_scr[j] / l_scr[j]).astype(o_ref.dtype)


def _fox_prompt(qkv, c, out_buf, *, heads, n_seq, seq_len, tq, hps, sub):
    m, width = qkv.shape
    d = width // 3
    hd = d // heads
    nq = seq_len // tq
    ng = heads // hps
    return pl.pallas_call(
        functools.partial(_fox_prompt_kernel, hps=hps, sub=sub), grid=(n_seq, ng, nq),
        in_specs=[pl.BlockSpec((tq, hps * hd), lambda b, g, i: (b * nq + i, g)),
                  pl.BlockSpec((seq_len, hps * hd), lambda b, g, i: (b, ng + g), pipeline_mode=pl.Buffered(1)),
                  pl.BlockSpec((seq_len, hps * hd), lambda b, g, i: (b, 2 * ng + g), pipeline_mode=pl.Buffered(1)),
                  pl.BlockSpec((seq_len, 128), lambda b, g, i: (b, 0), pipeline_mode=pl.Buffered(1)),
                  pl.BlockSpec(memory_space=pl.ANY)],
        out_specs=pl.BlockSpec((tq, hps * hd), lambda b, g, i: (b * nq + i, g)),
        out_shape=jax.ShapeDtypeStruct((m, d), BF16), input_output_aliases={4: 0},
        scratch_shapes=[pltpu.VMEM((hps, seq_len, 2 * hd), BF16), pltpu.VMEM((hps, hd, seq_len), BF16),
                        pltpu.VMEM((hps, 1, tq), F32), pltpu.VMEM((hps, 1, tq), F32),
                        pltpu.VMEM((hps, hd, tq), F32)],
        compiler_params=_params("arbitrary", "arbitrary", "arbitrary"), name="fox_prompt")(
            qkv, qkv, qkv, c, out_buf)


def _fox_sample_kernel(q_ref, kn_ref, vn_ref, c_ref, kc_ref, vc_ref, lfc_ref, prev_ref, o_ref, d_scr, ct_scr):
    del prev_ref
    g = pl.program_id(1)
    past, hpg, hd = kc_ref.shape
    ts = q_ref.shape[0]

    @pl.when(g == 0)
    def _():
        later = (lax.broadcasted_iota(jnp.int32, (past, past), 0)
                 > lax.broadcasted_iota(jnp.int32, (past, past), 1)).astype(BF16)
        hi, mid, lo = _split3(lfc_ref[...])
        d_scr[...] = _dot(hi, later) + _dot(mid, later) + _dot(lo, later)
        ct_scr[...] = _transpose_exact(c_ref[...])

    c_blk = c_ref[...]
    row = lax.broadcasted_iota(jnp.int32, (ts, ts), 0)
    col = lax.broadcasted_iota(jnp.int32, (ts, ts), 1)
    for j in range(hpg):
        h = g * hpg + j
        cq = _select_lane(c_blk, h)
        d_row = d_scr[pl.ds(h, 1), :]
        cn_row = ct_scr[pl.ds(h, 1), :]
        sl = slice(j * hd, (j + 1) * hd)
        q = q_ref[:, sl]
        s_past = _dg(q, kc_ref[:, j, :].astype(BF16), NT) + (cq + d_row) * LOG2E
        s_new = _dg(q, kn_ref[:, sl], NT) + (cq - cn_row) * LOG2E
        s_new = jnp.where(col <= row, s_new, -jnp.inf)
        m = jnp.maximum(jnp.max(s_past, axis=1, keepdims=True), jnp.max(s_new, axis=1, keepdims=True))
        p_past = jnp.exp2(s_past - m)
        p_new = jnp.exp2(s_new - m)
        den = jnp.sum(p_past, axis=1, keepdims=True) + jnp.sum(p_new, axis=1, keepdims=True)
        o = _dot(p_past.astype(BF16), vc_ref[:, j, :].astype(BF16)) + _dot(p_new.astype(BF16), vn_ref[:, sl])
        o_ref[:, sl] = (o / den).astype(o_ref.dtype)


def _fox_sample(qkv, c, k_cache, v_cache, lf_cache_t, prev_out, *, heads, row0, ts, hpg=8):
    m, width = qkv.shape
    d = width // 3
    hd = d // heads
    n_streams, past = k_cache.shape[:2]
    base = row0 // ts
    ng = heads // hpg
    return pl.pallas_call(
        _fox_sample_kernel, grid=(n_streams, ng),
        in_specs=[pl.BlockSpec((ts, hpg * hd), lambda b, g: (base + b, g)),
                  pl.BlockSpec((ts, hpg * hd), lambda b, g: (base + b, ng + g)),
                  pl.BlockSpec((ts, hpg * hd), lambda b, g: (base + b, 2 * ng + g)),
                  pl.BlockSpec((ts, 128), lambda b, g: (b, 0)),
                  pl.BlockSpec((None, past, hpg, hd), lambda b, g: (b, 0, g, 0)),
                  pl.BlockSpec((None, past, hpg, hd), lambda b, g: (b, 0, g, 0)),
                  pl.BlockSpec((None, heads, past), lambda b, g: (b, 0, 0)),
                  pl.BlockSpec(memory_space=pl.ANY)],
        out_specs=pl.BlockSpec((ts, hpg * hd), lambda b, g: (base + b, g)),
        out_shape=jax.ShapeDtypeStruct((m, d), BF16),
        scratch_shapes=[pltpu.VMEM((heads, past), F32), pltpu.VMEM((128, ts), F32)],
        input_output_aliases={7: 0},
        compiler_params=_params("arbitrary", "arbitrary"), name="fox_sample")(
            qkv, qkv, qkv, c, k_cache, v_cache, lf_cache_t, prev_out)


def _pad_cols(w, n):
    return jnp.pad(w, ((0, 0), (0, n - w.shape[1])))


def _row_vec(v, n=None):
    v = v.astype(F32).reshape(1, -1)
    return v if n is None else _pad_cols(v, n)


def _block_diag(blocks, copies):
    g, p, _ = blocks.shape
    eye = jnp.eye(copies, dtype=blocks.dtype)
    return jnp.einsum("ab,gpq->gapbq", eye, blocks).reshape(g, copies * p, copies * p)


def _gmlp_spatial(w_s, b_s, block, group_dim):
    pos = jnp.arange(block)
    mask = (pos[None, :] // GMLP_CHUNK) <= (pos[:, None] // GMLP_CHUNK)
    ws = jnp.where(mask[None], w_s[:, :block, :block], 0.0)
    copies = ROW_TILE // block
    bias = jnp.repeat(jnp.tile(jnp.transpose(b_s[:, :block]), (copies, 1)), group_dim, axis=1)
    return _block_diag(ws, copies).astype(BF16), bias.astype(F32)


def _cumsum_matrix(size, block):
    r = jnp.arange(size)
    return ((r[None, :] <= r[:, None]) & (r[None, :] // block == r[:, None] // block)).astype(BF16)


def kernel(x_prompt, x_sample, state_b_C, state_b_n, state_b_m, state_c_S, cache_d_k, cache_d_v, cache_d_logf, a_w_in, a_b_in, a_vn_g, a_vn_b, a_w_s, a_b_s, a_w_out, b_w_in, b_b_gates, b_norm_g, b_w_out, c_w_in, c_gn_g, c_gn_b, c_w_out, d_w_in, d_b_f, d_w_out, ffn_w_in, ffn_w_out, ln1_g, ln1_b, ln2_g, ln2_b):
    nbp, seq, d = x_prompt.shape
    nbs, ts, _ = x_sample.shape
    mp, ms = nbp * seq, nbs * ts
    past = cache_d_k.shape[2]
    depth = ffn_w_in.shape[0]
    alpha = (2.0 * depth) ** 0.25
    assert ms == ROW_TILE == SEQ_CHUNK and ROW_TILE % ts == 0 and seq % ATTN_BLOCK == 0 and seq % SEQ_CHUNK == 0 and seq % SUM_TILE == 0
    n_prompt_tiles = mp // ROW_TILE

    x = jnp.concatenate([x_prompt.reshape(mp, d), x_sample.reshape(ms, d)], axis=0)
    ffn_w_in_b, ffn_w_out_b = ffn_w_in.astype(BF16), ffn_w_out.astype(BF16)

    spent = None
    a_vs = []
    b_cp, b_np, b_mp, b_cs, b_ns, b_ms = [], [], [], [], [], []
    c_sp, c_ss = [], []
    d_kp, d_vp, d_fp, d_ks, d_vs, d_fs = [], [], [], [], [], []

    for i in range(depth):
        kind, j = i % 4, i // 4
        if kind == 0:
            w_in = a_w_in[j].astype(BF16)
            u, v, v_last = _gmlp_in(x, w_in[:, :d], _row_vec(a_b_in[j, :d]), w_in[:, d:], _row_vec(a_b_in[j, d:]),
                                    _row_vec(a_vn_g[j]), _row_vec(a_vn_b[j]))
            gd = d // GMLP_GROUPS
            wp, bp = _gmlp_spatial(a_w_s[j], a_b_s[j], GMLP_BLOCK, gd)
            wsm, bsm = _gmlp_spatial(a_w_s[j], a_b_s[j], ts, gd)
            mix = _gmlp_gate(u, v, jnp.stack([wp, wsm]), jnp.stack([bp, bsm]), n_prompt_tiles)
            w_out = a_w_out[j]
            a_vs.append(v_last[v_last.shape[0] - ms:].reshape(nbs, ts, d))
        elif kind == 1:
            heads = MLSTM_HEADS
            n_main = b_w_in.shape[2] - 2 * heads
            dqk = (n_main - 2 * d) // (2 * heads)
            qkv, gates = _qkv_proj(x, b_w_in[j, :, :n_main].astype(BF16),
                                   _pad_cols(b_w_in[j, :, n_main:], 128).astype(BF16), _row_vec(b_b_gates[j], 128),
                                   scaled_cols=heads * dqk, scale=dqk ** -0.5, n_raw=heads, tn=PROJ_WIDE_COLS)
            norm_g = _row_vec(b_norm_g[j])
            zc = jnp.zeros((nbp,) + state_b_C.shape[2:], F32)
            zn = jnp.zeros((nbp,) + state_b_n.shape[2:], F32)
            zm = jnp.zeros((nbp,) + state_b_m.shape[2:], F32)
            mix_s, cs, ns, ms_state = _mlstm(qkv, gates, norm_g, state_b_C[j], state_b_n[j], state_b_m[j],
                                             chunk=ts, row0=mp, seq_len=ts)
            mix, cp, np_, mp_state = _mlstm(qkv, gates, norm_g, zc, zn, zm, chunk=SEQ_CHUNK, row0=0, seq_len=seq,
                                            tail=mix_s)
            b_cp.append(cp); b_np.append(np_); b_mp.append(mp_state)
            b_cs.append(cs); b_ns.append(ns); b_ms.append(ms_state)
            w_out = b_w_out[j]
        elif kind == 2:
            heads = RET_HEADS
            dk = d // heads
            half = dk // 2
            inv = ROPE_BASE ** (-jnp.arange(half, dtype=F32) / half)
            ang_p = jnp.arange(seq, dtype=F32)[:, None] * inv[None, :]
            ang_s = (past + jnp.arange(ts)).astype(F32)[:, None] * inv[None, :]
            rows = lambda f: jnp.concatenate([jnp.tile(f(ang_p), (nbp, 1)), jnp.tile(f(ang_s), (nbs, 1))])
            qkvg = _rot_proj(x, c_w_in[j].astype(BF16), rows(jnp.cos), rows(jnp.sin),
                             n_q=d // PROJ_WIDE_COLS, n_k=d // PROJ_WIDE_COLS, k_scale=dk ** -0.5, head=dk,
                             tn=PROJ_WIDE_COLS)
            gn_g, gn_b = _row_vec(c_gn_g[j]), _row_vec(c_gn_b[j])
            zs = jnp.zeros((nbp,) + state_c_S.shape[2:], F32)
            mix_s, ss = _retention(qkvg, gn_g, gn_b, state_c_S[j], chunk=ts, row0=mp, seq_len=ts)
            mix, sp = _retention(qkvg, gn_g, gn_b, zs, chunk=SEQ_CHUNK, row0=0, seq_len=seq, tail=mix_s)
            c_sp.append(sp); c_ss.append(ss)
            w_out = c_w_out[j]
        else:
            heads = FOX_HEADS
            hd = d // heads
            qkv, logf, k_p, k_s, v_p, v_s = _qkv_proj(
                x, d_w_in[j, :, :3 * d].astype(BF16), _pad_cols(d_w_in[j, :, 3 * d:], 128).astype(BF16),
                _row_vec(d_b_f[j], 128), scaled_cols=d, scale=hd ** -0.5 * LOG2E, n_raw=0, kv_rows=(mp, ms))
            c = _running_sum(logf, _cumsum_matrix(SUM_TILE, SUM_TILE), row0=0, n_tiles=mp // SUM_TILE,
                             tiles_per_seq=seq // SUM_TILE)
            c_s = _running_sum(logf, _cumsum_matrix(ms, ts), row0=mp, n_tiles=1, tiles_per_seq=1)
            out_buf = spent if spent is not None else jnp.zeros((mp + ms, d), BF16)
            mix = _fox_prompt(qkv, c, out_buf, heads=heads, n_seq=nbp, seq_len=seq, tq=ATTN_BLOCK,
                              hps=ATTN_HEADS_PER_STEP, sub=ATTN_KEY_SUB)
            spent = None
            mix = _fox_sample(qkv, c_s, cache_d_k[j], cache_d_v[j],
                              jnp.transpose(cache_d_logf[j], (0, 2, 1)), mix, heads=heads, row0=mp, ts=ts)
            d_kp.append(k_p.reshape(nbp, seq, heads, hd)); d_ks.append(k_s.reshape(nbs, ts, heads, hd))
            d_vp.append(v_p.reshape(nbp, seq, heads, hd)); d_vs.append(v_s.reshape(nbs, ts, heads, hd))
            d_fp.append(logf[:mp, :heads].reshape(nbp, seq, heads))
            d_fs.append(logf[mp:, :heads].reshape(nbs, ts, heads))
            w_out = d_w_out[j]
        x = _out_ln(mix, w_out.astype(BF16), x, _row_vec(ln1_g[i]), _row_vec(ln1_b[i]), alpha)
        if mix.shape == (mp + ms, d):
            spent = mix
        x = _ffn(x, ffn_w_in_b, ffn_w_out_b, i, _row_vec(ln2_g[i]), _row_vec(ln2_b[i]), alpha,
                 final_rows=(mp, ms) if i == depth - 1 else None)

    y_prompt, y_sample = x
    return (y_prompt.reshape(nbp, seq, d), y_sample.reshape(nbs, ts, d), jnp.stack(a_vs),
            jnp.stack(b_cp), jnp.stack(b_np), jnp.stack(b_mp),
            jnp.stack(b_cs), jnp.stack(b_ns), jnp.stack(b_ms),
            jnp.stack(c_sp), jnp.stack(c_ss),
            jnp.stack(d_kp), jnp.stack(d_vp), jnp.stack(d_fp),
            jnp.stack(d_ks), jnp.stack(d_vs), jnp.stack(d_fs))
```

```python
import functools
import math

import jax
import jax.numpy as jnp
from jax import lax
from jax.experimental import pallas as pl
from jax.experimental.pallas import tpu as pltpu

F32 = jnp.float32
BF16 = jnp.bfloat16

LN_EPS = 1e-5
ROPE_BASE = 10000.0
GMLP_BLOCK = 128
GMLP_CHUNK = 64
GMLP_GROUPS = 8
MLSTM_HEADS = 8
RET_HEADS = 8
FOX_HEADS = 16
LANES = 128
ROW_TILE = 256
SUM_TILE = 1024
SEQ_CHUNK = 256
ATTN_BLOCK = 512
ATTN_HEADS_PER_STEP = 4
PROJ_WIDE_COLS = 2048
ATTN_KEY_SUB = 256
LOG2E = math.log2(math.e)
VMEM_LIMIT = 56 * 2 ** 20

NT = (((1,), (1,)), ((), ()))
TN = (((0,), (0,)), ((), ()))


def _params(*sem):
    return pltpu.CompilerParams(dimension_semantics=sem, vmem_limit_bytes=VMEM_LIMIT)


def _row_tile(m, candidates):
    for t in candidates:
        if m % t == 0:
            return t
    raise ValueError(f"no row tile for {m}")


def _dot(a, b):
    return jnp.dot(a, b, preferred_element_type=F32)


def _dg(a, b, dims):
    return lax.dot_general(a, b, dims, preferred_element_type=F32)


def _split3(x):
    hi = x.astype(BF16)
    r1 = x - hi.astype(F32)
    mid = r1.astype(BF16)
    lo = (r1 - mid.astype(F32)).astype(BF16)
    return hi, mid, lo


def _dot_exact_lhs(mat_bf16, x):
    hi, mid, lo = _split3(x)
    return _dot(mat_bf16, hi) + _dot(mat_bf16, mid) + _dot(mat_bf16, lo)


def _transpose_exact(x):
    n = x.shape[1]
    eye = (lax.broadcasted_iota(jnp.int32, (n, n), 0) == lax.broadcasted_iota(jnp.int32, (n, n), 1)).astype(BF16)
    hi, mid, lo = _split3(x)
    return _dg(eye, hi, NT) + _dg(eye, mid, NT) + _dg(eye, lo, NT)


def _sigmoid(x):
    return 1.0 / (1.0 + jnp.exp(-x))


def _log_sigmoid(x):
    return jnp.minimum(x, 0.0) - jnp.log1p(jnp.exp(-jnp.abs(x)))


def _gelu_tanh(x):
    return 0.5 * x * (1.0 + jnp.tanh(math.sqrt(2.0 / math.pi) * (x + 0.044715 * (x * x * x))))


def _layer_norm(y, g, b):
    mu = jnp.mean(y, axis=-1, keepdims=True)
    d = y - mu
    var = jnp.mean(d * d, axis=-1, keepdims=True)
    return d * lax.rsqrt(var + LN_EPS) * g + b


def _matmul_row_parts(lhs_ref, w_ref, epilogue, cast=False):
    tm = lhs_ref.shape[0]
    n_parts = 3 if tm % 48 == 0 else 2
    part = tm // n_parts
    rows = [slice(p * part, (p + 1) * part) for p in range(n_parts)]

    def matmul(r):
        lhs = lhs_ref[r, :]
        return _dot(lhs.astype(BF16) if cast else lhs, w_ref[...])

    acc = matmul(rows[0])
    for p, r in enumerate(rows):
        nxt = matmul(rows[p + 1]) if p + 1 < n_parts else None
        epilogue(r, acc)
        acc = nxt


def _cast_rows_once(x_ref, xb_scr, j):
    @pl.when(j == 0)
    def _():
        xb_scr[...] = x_ref[...].astype(BF16)


def _gmlp_in_kernel(x_ref, wu_ref, bu_ref, wv_ref, bv_ref, g_ref, be_ref, u_ref, v_ref, last_ref):
    def u_epilogue(r, acc):
        u_ref[r, :] = _gelu_tanh(acc + bu_ref[...]).astype(u_ref.dtype)

    def v_epilogue(r, acc):
        v = _layer_norm(_gelu_tanh(acc + bv_ref[...]), g_ref[...], be_ref[...])
        v_ref[r, :] = v.astype(v_ref.dtype)
        last_ref[r, :] = v

    _matmul_row_parts(x_ref, wu_ref, u_epilogue, cast=True)
    _matmul_row_parts(x_ref, wv_ref, v_epilogue, cast=True)


def _gmlp_in(x, w_u, b_u, w_v, b_v, vn_g, vn_b):
    m, d = x.shape
    tm = _row_tile(m, (384, 256))
    n = w_u.shape[1]
    row = pl.BlockSpec((tm, d), lambda i: (i, 0))
    wsp = pl.BlockSpec((d, n), lambda i: (0, 0), pipeline_mode=pl.Buffered(1))
    vec = pl.BlockSpec((1, n), lambda i: (0, 0))
    out = pl.BlockSpec((tm, n), lambda i: (i, 0))
    return pl.pallas_call(
        _gmlp_in_kernel, grid=(m // tm,), in_specs=[row, wsp, vec, wsp, vec, vec, vec],
        out_specs=[out, out, pl.BlockSpec((tm, n), lambda i: (0, 0))],
        out_shape=[jax.ShapeDtypeStruct((m, n), BF16), jax.ShapeDtypeStruct((m, n), BF16),
                   jax.ShapeDtypeStruct((tm, n), F32)],
        compiler_params=_params("arbitrary"), name="gmlp_in")(x, w_u, b_u, w_v, b_v, vn_g, vn_b)


def _qkv_proj_kernel(x_ref, w_ref, wg_ref, bg_ref, o_ref, gate_ref, *refs, scaled_cols, scale, nb, tail, n_raw):
    kv_refs, xb_scr = refs[:-1], refs[-1]
    j = pl.program_id(1)
    _cast_rows_once(x_ref, xb_scr, j)

    @pl.when(j == 0)
    def _():
        g = _dot(xb_scr[...], wg_ref[...]) + bg_ref[...]
        lane = lax.broadcasted_iota(jnp.int32, g.shape, 1)
        gate_ref[...] = jnp.where(lane < n_raw, g, _log_sigmoid(g))

    tn = o_ref.shape[1]
    col = j * tn + lax.broadcasted_iota(jnp.int32, (1, tn), 1)
    col_scale = jnp.where(col < scaled_cols, scale, 1.0)
    accs = []

    def epilogue(r, acc):
        o_ref[r, :] = (acc * col_scale).astype(o_ref.dtype)
        accs.append(acc)

    _matmul_row_parts(xb_scr, w_ref, epilogue)

    if kv_refs:
        kp_ref, ks_ref, vp_ref, vs_ref = kv_refs
        tm, tn = o_ref.shape
        acc = jnp.concatenate(accs, axis=0)

        @pl.when(jnp.logical_and(j >= nb, j < 2 * nb))
        def _():
            kp_ref[...] = acc
            ks_ref[:, pl.ds(pl.multiple_of((j - nb) * tn, tn), tn)] = acc[tm - tail:]

        @pl.when(j >= 2 * nb)
        def _():
            vp_ref[...] = acc
            vs_ref[:, pl.ds(pl.multiple_of((j - 2 * nb) * tn, tn), tn)] = acc[tm - tail:]


def _qkv_proj(x, w, w_gate, b_gate, *, scaled_cols, scale, n_raw, kv_rows=None, tn=1024):
    m, d = x.shape
    n = w.shape[1]
    tm = _row_tile(m, (768, 512, 256))
    in_specs = [pl.BlockSpec((tm, d), lambda i, j: (i, 0)), pl.BlockSpec((d, tn), lambda i, j: (0, j)),
                pl.BlockSpec((d, 128), lambda i, j: (0, 0)), pl.BlockSpec((1, 128), lambda i, j: (0, 0))]
    out_specs = [pl.BlockSpec((tm, tn), lambda i, j: (i, j)), pl.BlockSpec((tm, 128), lambda i, j: (i, 0))]
    out_shape = [jax.ShapeDtypeStruct((m, n), BF16), jax.ShapeDtypeStruct((m, 128), F32)]
    nb, tail = n // (3 * tn), 0
    if kv_rows is not None:
        mp, tail = kv_rows
        assert mp + tail == m and tail <= tm
        for first in (nb, 2 * nb):
            col = functools.partial(lambda j, first: jnp.clip(j - first, 0, nb - 1), first=first)
            out_specs += [pl.BlockSpec((tm, tn), lambda i, j, col=col: (i, col(j))),
                          pl.BlockSpec((tail, n // 3), lambda i, j: (0, 0))]
            out_shape += [jax.ShapeDtypeStruct((mp, n // 3), F32), jax.ShapeDtypeStruct((tail, n // 3), F32)]
    return pl.pallas_call(
        functools.partial(_qkv_proj_kernel, scaled_cols=scaled_cols, scale=scale, nb=nb, tail=tail, n_raw=n_raw),
        grid=(m // tm, n // tn), in_specs=in_specs, out_specs=out_specs, out_shape=out_shape,
        scratch_shapes=[pltpu.VMEM((tm, d), BF16)],
        compiler_params=_params("arbitrary", "arbitrary"), name="qkv_proj")(x, w, w_gate, b_gate)


def _rot_proj_kernel(x_ref, w_ref, cos_ref, sin_ref, o_ref, xb_scr, *, n_q, n_k, k_scale, head):
    j = pl.program_id(1)
    _cast_rows_once(x_ref, xb_scr, j)
    half = head // 2
    rotated = j < n_q + n_k
    col_scale = jnp.where(jnp.logical_and(j >= n_q, rotated), k_scale, 1.0).astype(F32)

    def epilogue(r, acc):
        cos = jnp.where(rotated, cos_ref[r, :], 1.0)
        sin = jnp.where(rotated, sin_ref[r, :], 0.0)
        for h in range(acc.shape[1] // head):
            x1 = acc[:, h * head:h * head + half]
            x2 = acc[:, h * head + half:(h + 1) * head]
            o_ref[r, h * head:h * head + half] = ((x1 * cos - x2 * sin) * col_scale).astype(o_ref.dtype)
            o_ref[r, h * head + half:(h + 1) * head] = ((x1 * sin + x2 * cos) * col_scale).astype(o_ref.dtype)

    _matmul_row_parts(xb_scr, w_ref, epilogue)


def _rot_proj(x, w, cos, sin, *, n_q, n_k, k_scale, head, tn=1024):
    m, d = x.shape
    n = w.shape[1]
    tm = _row_tile(m, (768, 512, 256))
    half = head // 2
    return pl.pallas_call(
        functools.partial(_rot_proj_kernel, n_q=n_q, n_k=n_k, k_scale=k_scale, head=head),
        grid=(m // tm, n // tn),
        in_specs=[pl.BlockSpec((tm, d), lambda i, j: (i, 0)), pl.BlockSpec((d, tn), lambda i, j: (0, j)),
                  pl.BlockSpec((tm, half), lambda i, j: (i, 0)), pl.BlockSpec((tm, half), lambda i, j: (i, 0))],
        out_specs=pl.BlockSpec((tm, tn), lambda i, j: (i, j)),
        out_shape=jax.ShapeDtypeStruct((m, n), BF16), scratch_shapes=[pltpu.VMEM((tm, d), BF16)],
        compiler_params=_params("arbitrary", "arbitrary"), name="rot_proj")(x, w, cos, sin)


def _running_sum_kernel(lf_ref, tri_ref, c_ref, carry_ref, *, tiles_per_seq):
    i = pl.program_id(0)

    @pl.when(i % tiles_per_seq == 0)
    def _():
        carry_ref[...] = jnp.zeros_like(carry_ref)

    c = _dot_exact_lhs(tri_ref[...], lf_ref[...]) + carry_ref[...]
    c_ref[...] = c
    carry_ref[...] = c[-1:, :]


def _running_sum(lf, tri, *, row0, n_tiles, tiles_per_seq):
    tile = tri.shape[0]
    return pl.pallas_call(
        functools.partial(_running_sum_kernel, tiles_per_seq=tiles_per_seq), grid=(n_tiles,),
        in_specs=[pl.BlockSpec((tile, 128), lambda i: (row0 // tile + i, 0)),
                  pl.BlockSpec((tile, tile), lambda i: (0, 0))],
        out_specs=pl.BlockSpec((tile, 128), lambda i: (i, 0)),
        out_shape=jax.ShapeDtypeStruct((n_tiles * tile, 128), F32),
        scratch_shapes=[pltpu.VMEM((1, 128), F32)],
        compiler_params=_params("arbitrary"), name="running_sum")(lf, tri)


def _out_ln_kernel(h_ref, w_ref, x_ref, g_ref, b_ref, o_ref, *, alpha):
    def epilogue(r, acc):
        o_ref[r, :] = _layer_norm(alpha * x_ref[r, :] + acc, g_ref[...], b_ref[...])

    _matmul_row_parts(h_ref, w_ref, epilogue)


def _out_ln(h, w, x, g, b, alpha):
    m, k = h.shape
    d = w.shape[1]
    tm = _row_tile(m, (768, 384, 256) if k <= d else (384, 256))
    vec = pl.BlockSpec((1, d), lambda i: (0, 0))
    row = pl.BlockSpec((tm, d), lambda i: (i, 0))
    return pl.pallas_call(
        functools.partial(_out_ln_kernel, alpha=alpha), grid=(m // tm,),
        in_specs=[pl.BlockSpec((tm, k), lambda i: (i, 0)),
                  pl.BlockSpec((k, d), lambda i: (0, 0), pipeline_mode=pl.Buffered(1)),
                  row, vec, vec],
        out_specs=row, out_shape=jax.ShapeDtypeStruct((m, d), F32),
        compiler_params=_params("arbitrary"), name="out_ln")(h, w, x, g, b)


def _ffn_kernel(x_ref, wg_ref, wu_ref, wo_ref, g_ref, b_ref, o_ref, *rest, alpha):
    tail_refs, xb_scr = rest[:-1], rest[-1]
    j = pl.program_id(1)
    _cast_rows_once(x_ref, xb_scr, j)

    @pl.when(j == 0)
    def _():
        o_ref[...] = jnp.zeros_like(o_ref)

    xb = xb_scr[...]
    gate = _dot(xb, wg_ref[...])
    up = _dot(xb, wu_ref[...])
    hid = (gate * _sigmoid(gate) * up).astype(BF16)
    o_ref[...] += _dot(hid, wo_ref[...])

    @pl.when(j == pl.num_programs(1) - 1)
    def _():
        y = _layer_norm(alpha * x_ref[...] + o_ref[...], g_ref[...], b_ref[...])
        o_ref[...] = y
        for t_ref in tail_refs:
            t_ref[...] = y[y.shape[0] - t_ref.shape[0]:]


def _ffn(x, w_in, w_out, layer, g, b, alpha, final_rows=None, th=512):
    m, d = x.shape
    hidden = w_out.shape[1]
    tm = _row_tile(m, (768, 512, 256))
    nh = hidden // th
    vec = pl.BlockSpec((1, d), lambda i, j: (0, 0))
    row = pl.BlockSpec((tm, d), lambda i, j: (i, 0))
    if final_rows is None:
        out_specs, out_shape = row, jax.ShapeDtypeStruct((m, d), F32)
    else:
        mp, tail = final_rows
        assert mp + tail == m and tail <= tm
        out_specs = [row, pl.BlockSpec((tail, d), lambda i, j: (0, 0))]
        out_shape = [jax.ShapeDtypeStruct((mp, d), F32), jax.ShapeDtypeStruct((tail, d), F32)]
    return pl.pallas_call(
        functools.partial(_ffn_kernel, alpha=alpha), grid=(m // tm, nh),
        in_specs=[row, pl.BlockSpec((None, d, th), lambda i, j: (layer, 0, j)),
                  pl.BlockSpec((None, d, th), lambda i, j: (layer, 0, j + nh)),
                  pl.BlockSpec((None, th, d), lambda i, j: (layer, j, 0)), vec, vec],
        out_specs=out_specs, out_shape=out_shape,
        scratch_shapes=[pltpu.VMEM((tm, d), BF16)],
        compiler_params=_params("arbitrary", "arbitrary"), name="ffn")(x, w_in, w_in, w_out, g, b)


def _gmlp_gate_kernel(u_ref, v_ref, w_ref, b_ref, y_ref, *, groups):
    gd = u_ref.shape[1] // groups
    for g in range(groups):
        sl = slice(g * gd, (g + 1) * gd)
        s = _dot(w_ref[g], v_ref[:, sl]) + b_ref[:, sl]
        y_ref[:, sl] = (u_ref[:, sl].astype(F32) * s).astype(y_ref.dtype)


def _gmlp_gate(u, v, w_sp, b_sp, n_prompt_tiles):
    m, d = u.shape
    tm = ROW_TILE
    groups = w_sp.shape[1]
    row = pl.BlockSpec((tm, d), lambda i: (i, 0))
    return pl.pallas_call(
        functools.partial(_gmlp_gate_kernel, groups=groups), grid=(m // tm,),
        in_specs=[row, row,
                  pl.BlockSpec((None, groups, tm, tm), lambda i: (i // n_prompt_tiles, 0, 0, 0)),
                  pl.BlockSpec((None, tm, d), lambda i: (i // n_prompt_tiles, 0, 0))],
        out_specs=row, out_shape=jax.ShapeDtypeStruct((m, d), BF16),
        compiler_params=_params("arbitrary"), name="gmlp_gate")(u, v, w_sp, b_sp)


def _chunked_mixer_kernel(step_fn, n_in, n_out, n_chunks, n_main, *refs):
    ins, outs = refs[:n_in], refs[len(refs) - n_out:]
    s = pl.program_id(0)
    if len(refs) > n_in + n_out:
        @pl.when(s == n_main)
        def _():
            outs[0][...] = refs[n_in][...]

    @pl.when(s < n_main)
    def _():
        step_fn(*ins, *outs, first=(s % n_chunks == 0))


def _mlstm_step(qkv_ref, gt_ref, ng_ref, c0_ref, n0_ref, m0_ref, h_ref, c_ref, n_ref, m_ref, *, first, heads):
    d = qkv_ref.shape[1] // 3
    q_ref, k_ref = qkv_ref.at[:, 0:d // 2], qkv_ref.at[:, d // 2:d]
    v_ref, og_ref = qkv_ref.at[:, d:2 * d], qkv_ref.at[:, 2 * d:3 * d]
    L = q_ref.shape[0]
    dqk = q_ref.shape[1] // heads
    dv = v_ref.shape[1] // heads

    @pl.when(first)
    def _():
        c_ref[...] = c0_ref[...]
        n_ref[...] = n0_ref[...]
        m_ref[...] = m0_ref[...]

    g = gt_ref[...]
    row = lax.broadcasted_iota(jnp.int32, (L, L), 0)
    col = lax.broadcasted_iota(jnp.int32, (L, L), 1)
    causal = col <= row
    bc = _dot_exact_lhs(causal.astype(BF16), g)
    g_t = _transpose_exact(g)
    bc_t = _transpose_exact(bc)

    ones_cols = jnp.ones((L, LANES), BF16)
    mean_cols = jnp.full((dv, LANES), 1.0 / dv, BF16)

    def rep(col):
        return jnp.broadcast_to(col, (L, LANES))

    def wide(x_rep, cols):
        return x_rep[:, :cols] if cols < LANES else jnp.concatenate([x_rep] * (cols // LANES), axis=1)

    for h in range(heads):
        a = rep(bc[:, heads + h:heads + h + 1])
        li = rep(g[:, h:h + 1])
        a_row = bc_t[heads + h:heads + h + 1, :]
        li_row = g_t[h:h + 1, :]
        m_prev = m_ref[h:h + 1, :]
        dmat = jnp.where(causal, wide(a, L) - a_row + li_row, -jnp.inf)
        inter = a + m_prev
        m_t = jnp.maximum(inter, rep(jnp.max(dmat, axis=1, keepdims=True)))
        w_intra = jnp.exp(dmat - wide(m_t, L))
        w_inter = jnp.exp(inter - m_t)
        qh = q_ref[:, h * dqk:(h + 1) * dqk]
        kh = k_ref[:, h * dqk:(h + 1) * dqk]
        vh = v_ref[:, h * dv:(h + 1) * dv]
        c_old = c_ref[h]
        n_old = n_ref[h:h + 1, :]
        scores = (_dg(qh, kh, NT) * w_intra).astype(BF16)
        v_ones = jnp.concatenate([vh, ones_cols], axis=1)
        c_n = jnp.concatenate([c_old, jnp.broadcast_to(n_old, (LANES, dqk))], axis=0).astype(BF16)
        both = _dot(scores, v_ones) + wide(w_inter, dv + LANES) * _dg(qh, c_n, NT)
        inv = 1.0 / jnp.maximum(jnp.abs(both[:, dv:]), jnp.exp(-m_t))
        hid = both[:, :dv] * wide(inv, dv)
        mean_sq = _dot((hid * hid).astype(BF16), mean_cols)
        sl = slice(h * dv, (h + 1) * dv)
        y = hid * wide(lax.rsqrt(mean_sq + LN_EPS), dv) * ng_ref[:, sl]
        h_ref[:, sl] = (y * _sigmoid(og_ref[:, sl].astype(F32))).astype(h_ref.dtype)

        m_new = m_t[L - 1:L, :]
        b_last = a[L - 1:L, :]
        w_c = jnp.exp(b_last + m_prev - m_new)
        w_s = jnp.exp(b_last - a + li - m_new)
        c_ref[h] = w_c * c_old + _dg((vh.astype(F32) * wide(w_s, dv)).astype(BF16), kh, TN)
        n_ref[h:h + 1, :] = w_c * n_old + jnp.sum(kh.astype(F32) * w_s, axis=0, keepdims=True)
        m_ref[h:h + 1, :] = m_new[:, 0:1]


def _mlstm(qkv, gates, norm_g, c0, n0, m0, *, chunk, row0, seq_len, tail=None):
    width = qkv.shape[1]
    d = width // 3
    nb, heads, dv, dqk = c0.shape
    nc = seq_len // chunk
    base = row0 // chunk
    n_main = nb * nc

    def row(s):
        return base + jnp.minimum(s, n_main - 1)

    def seq(s):
        return jnp.minimum(s // nc, nb - 1)

    state_specs = [pl.BlockSpec((None, heads, dv, dqk), lambda s: (seq(s), 0, 0, 0)),
                   pl.BlockSpec((None, heads, dqk), lambda s: (seq(s), 0, 0)),
                   pl.BlockSpec((None, heads, 1), lambda s: (seq(s), 0, 0))]
    in_specs = [pl.BlockSpec((chunk, width), lambda s: (row(s), 0)),
                pl.BlockSpec((chunk, 128), lambda s: (row(s), 0)),
                pl.BlockSpec((1, d), lambda s: (0, 0))] + state_specs
    args = [qkv, gates, norm_g, c0, n0, m0.reshape(nb, heads, 1)]
    if tail is not None:
        assert tail.shape == (chunk, d)
        in_specs.append(pl.BlockSpec((chunk, d), lambda s: (0, 0)))
        args.append(tail)
    n_steps = n_main + (tail is not None)
    outs = pl.pallas_call(
        functools.partial(_chunked_mixer_kernel, functools.partial(_mlstm_step, heads=heads), 6, 4, nc, n_main),
        grid=(n_steps,), in_specs=in_specs,
        out_specs=[pl.BlockSpec((chunk, d), lambda s: (s, 0))] + state_specs,
        out_shape=[jax.ShapeDtypeStruct((n_steps * chunk, d), BF16), jax.ShapeDtypeStruct(c0.shape, F32),
                   jax.ShapeDtypeStruct(n0.shape, F32), jax.ShapeDtypeStruct((nb, heads, 1), F32)],
        compiler_params=_params("arbitrary"), name="mlstm")(*args)
    return outs[0], outs[1], outs[2], outs[3].reshape(nb, heads)


def _retention_step(qkvg_ref, gg_ref, gb_ref, s0_ref, o_ref, s_ref, *, first, heads):
    d = qkvg_ref.shape[1] // 6
    q_ref, k_ref = qkvg_ref.at[:, 0:d], qkvg_ref.at[:, d:2 * d]
    v_ref, g_ref = qkvg_ref.at[:, 2 * d:4 * d], qkvg_ref.at[:, 4 * d:6 * d]
    L = q_ref.shape[0]
    dk = q_ref.shape[1] // heads
    dv = v_ref.shape[1] // heads

    @pl.when(first)
    def _():
        s_ref[...] = s0_ref[...]

    row = lax.broadcasted_iota(jnp.int32, (L, L), 0)
    col = lax.broadcasted_iota(jnp.int32, (L, L), 1)
    causal = col <= row
    lag = (row - col).astype(F32)
    t = lax.broadcasted_iota(jnp.int32, (L, 1), 0).astype(F32)

    for h in range(heads):
        log_gamma = math.log1p(-(2.0 ** (-5.0 - h)))
        decay = jnp.where(causal, jnp.exp(lag * log_gamma), 0.0)
        decay_q = jnp.exp((t + 1.0) * log_gamma)
        decay_k = jnp.exp((L - 1.0 - t) * log_gamma)
        decay_s = math.exp(L * log_gamma)
        qh = q_ref[:, h * dk:(h + 1) * dk]
        kh = k_ref[:, h * dk:(h + 1) * dk]
        vh = v_ref[:, h * dv:(h + 1) * dv]
        s_old = s_ref[h]
        scores = _dg(qh, kh, NT) * decay
        o = _dot(scores.astype(BF16), vh) + _dot(qh, s_old.astype(BF16)) * decay_q
        s_ref[h] = decay_s * s_old + _dg((kh.astype(F32) * decay_k).astype(BF16), vh, TN)
        sl = slice(h * dv, (h + 1) * dv)
        y = _layer_norm(o, gg_ref[:, sl], gb_ref[:, sl])
        gate = g_ref[:, sl].astype(F32)
        o_ref[:, sl] = (gate * _sigmoid(gate) * y).astype(o_ref.dtype)


def _retention(qkvg, gn_g, gn_b, s0, *, chunk, row0, seq_len, tail=None):
    width = qkvg.shape[1]
    d = width // 6
    nb, heads, dk, dv = s0.shape
    nc = seq_len // chunk
    base = row0 // chunk
    n_main = nb * nc
    state_spec = pl.BlockSpec((None, heads, dk, dv), lambda s: (jnp.minimum(s // nc, nb - 1), 0, 0, 0))
    in_specs = [pl.BlockSpec((chunk, width), lambda s: (base + jnp.minimum(s, n_main - 1), 0)),
                pl.BlockSpec((1, 2 * d), lambda s: (0, 0)),
                pl.BlockSpec((1, 2 * d), lambda s: (0, 0)), state_spec]
    args = [qkvg, gn_g, gn_b, s0]
    if tail is not None:
        assert tail.shape == (chunk, 2 * d)
        in_specs.append(pl.BlockSpec((chunk, 2 * d), lambda s: (0, 0)))
        args.append(tail)
    n_steps = n_main + (tail is not None)
    return pl.pallas_call(
        functools.partial(_chunked_mixer_kernel, functools.partial(_retention_step, heads=heads), 4, 2, nc, n_main),
        grid=(n_steps,), in_specs=in_specs,
        out_specs=[pl.BlockSpec((chunk, 2 * d), lambda s: (s, 0)), state_spec],
        out_shape=[jax.ShapeDtypeStruct((n_steps * chunk, 2 * d), BF16), jax.ShapeDtypeStruct(s0.shape, F32)],
        compiler_params=_params("arbitrary"), name="retention")(*args)


def _select_lane(block, lane_index):
    lane = lax.broadcasted_iota(jnp.int32, block.shape, 1)
    return jnp.sum(jnp.where(lane == lane_index, block, 0.0), axis=1, keepdims=True)


def _bias_tail(c_col, *, query):
    hi, mid, lo = (t.astype(F32) for t in _split3(c_col))
    lane = lax.broadcasted_iota(jnp.int32, (c_col.shape[0], 128), 1)
    if query:
        tail = jnp.where(lane == 0, hi, jnp.where(lane == 1, mid, jnp.where(lane == 2, lo,
                                                                             jnp.where(lane < 6, 1.0, 0.0))))
    else:
        tail = jnp.where(lane < 3, 1.0, jnp.where(lane == 3, -hi, jnp.where(lane == 4, -mid,
                                                                             jnp.where(lane == 5, -lo, 0.0))))
    return tail.astype(BF16)


def _fox_prompt_kernel(q_ref, k_ref, v_ref, c_ref, buf_ref, o_ref, ka_scr, vt_scr, m_scr, l_scr, acc_scr, *,
                       hps, sub):
    del buf_ref
    hg = pl.program_id(1)
    qi = pl.program_id(2)
    tq = tk = q_ref.shape[0]
    seq = k_ref.shape[0]
    hd = q_ref.shape[1] // hps

    @pl.when(qi == 0)
    def _():
        eye =(lax.broadcasted_iota(jnp.int32, (hd, hd), 0)
               == lax.broadcasted_iota(jnp.int32, (hd, hd), 1)).astype(BF16)

        def build(r, carry):
            r0 = pl.multiple_of(r * tk, tk)
            cblk = c_ref[pl.ds(r0, tk), :] * LOG2E
            for j in range(hps):
                ka_scr[j, pl.ds(r0, tk), 0:hd] = k_ref[pl.ds(r0, tk), j * hd:(j + 1) * hd]
                ka_scr[j, pl.ds(r0, tk), hd:2 * hd] = _bias_tail(_select_lane(cblk, hg * hps + j), query=False)
                vt_scr[j, :, pl.ds(r0, tk)] = _dg(eye, v_ref[pl.ds(r0, tk), j * hd:(j + 1) * hd], NT).astype(BF16)
            return carry
        lax.fori_loop(0, seq // tk, build, 0)

    cq_blk = c_ref[pl.ds(pl.multiple_of(qi * tq, tq), tq), :] * LOG2E
    qa = [jnp.concatenate([q_ref[:, j * hd:(j + 1) * hd],
                           _bias_tail(_select_lane(cq_blk, hg * hps + j), query=True)], axis=1)
          for j in range(hps)]
    eye2 = (lax.broadcasted_iota(jnp.int32, (2 * hd, 2 * hd), 0)
            == lax.broadcasted_iota(jnp.int32, (2 * hd, 2 * hd), 1)).astype(BF16)
    qa_t = [_dg(eye2, qa[j], NT).astype(BF16) for j in range(hps)]
    m_scr[...] = jnp.full_like(m_scr, -jnp.inf)
    l_scr[...] = jnp.zeros_like(l_scr)
    acc_scr[...] = jnp.zeros_like(acc_scr)

    def logits(j, ks):
        return _dot(ka_scr[j, pl.ds(ks, sub), :], qa_t[j])

    def absorb(j, st, ks, diag_offset):
        if diag_offset is not None:
            key = lax.broadcasted_iota(jnp.int32, st.shape, 0)
            qry = lax.broadcasted_iota(jnp.int32, st.shape, 1)
            st = jnp.where(key + diag_offset <= qry, st, -jnp.inf)
        m_old = m_scr[j]
        m_new = jnp.maximum(m_old, jnp.max(st, axis=0, keepdims=True))
        pt = jnp.exp2(st - m_new)
        alpha = jnp.exp2(m_old - m_new)
        l_scr[j] = alpha * l_scr[j] + jnp.sum(pt, axis=0, keepdims=True)
        acc_scr[j] = alpha * acc_scr[j] + _dot(vt_scr[j, :, pl.ds(ks, sub)], pt.astype(BF16))
        m_scr[j] = m_new

    def key_block(k0, st_cur, masked):
        units = [(j, d) for d in range(tk // sub) for j in range(hps)]
        start = lambda d: pl.multiple_of(k0 + d * sub, sub)
        for n, (j, d) in enumerate(units):
            if n + 1 < len(units):
                st_next = logits(units[n + 1][0], start(units[n + 1][1]))
            else:
                st_next = None if masked else logits(units[0][0], pl.multiple_of(k0 + tk, sub))
            absorb(j, st_cur, start(d), d * sub if masked else None)
            st_cur = st_next
        return st_cur

    def below_diagonal(kk, st):
        return key_block(pl.multiple_of(kk * tk, tk), st, False)

    st_diag = lax.fori_loop(0, qi, below_diagonal, logits(0, 0))
    key_block(pl.multiple_of(qi * tk, tk), st_diag, True)
    for j in range(hps):
        o_ref[:, j * hd:(j + 1) * hd] = jnp.transpose(acc_scr[j] / l_scr[j]).astype(o_ref.dtype)


def _fox_prompt(qkv, c, out_buf, *, heads, n_seq, seq_len, tq, hps, sub):
    m, width = qkv.shape
    d = width // 3
    hd = d // heads
    nq = seq_len // tq
    ng = heads // hps
    return pl.pallas_call(
        functools.partial(_fox_prompt_kernel, hps=hps, sub=sub), grid=(n_seq, ng, nq),
        in_specs=[pl.BlockSpec((tq, hps * hd), lambda b, g, i: (b * nq + i, g)),
                  pl.BlockSpec((seq_len, hps * hd), lambda b, g, i: (b, ng + g), pipeline_mode=pl.Buffered(1)),
                  pl.BlockSpec((seq_len, hps * hd), lambda b, g, i: (b, 2 * ng + g), pipeline_mode=pl.Buffered(1)),
                  pl.BlockSpec((seq_len, 128), lambda b, g, i: (b, 0), pipeline_mode=pl.Buffered(1)),
                  pl.BlockSpec(memory_space=pl.ANY)],
        out_specs=pl.BlockSpec((tq, hps * hd), lambda b, g, i: (b * nq + i, g)),
        out_shape=jax.ShapeDtypeStruct((m, d), BF16), input_output_aliases={4: 0},
        scratch_shapes=[pltpu.VMEM((hps, seq_len, 2 * hd), BF16), pltpu.VMEM((hps, hd, seq_len), BF16),
                        pltpu.VMEM((hps, 1, tq), F32), pltpu.VMEM((hps, 1, tq), F32),
                        pltpu.VMEM((hps, hd, tq), F32)],
        compiler_params=_params("arbitrary", "arbitrary", "arbitrary"), name="fox_prompt")(
            qkv, qkv, qkv, c, out_buf)


def _fox_sample_kernel(q_ref, kn_ref, vn_ref, c_ref, kc_ref, vc_ref, lfc_ref, prev_ref, o_ref, d_scr, ct_scr):
    del prev_ref
    g = pl.program_id(1)
    past, hpg, hd = kc_ref.shape
    ts = q_ref.shape[0]

    @pl.when(g == 0)
    def _():
        later = (lax.broadcasted_iota(jnp.int32, (past, past), 0)
                 > lax.broadcasted_iota(jnp.int32, (past, past), 1)).astype(BF16)
        hi, mid, lo = _split3(lfc_ref[...])
        d_scr[...] = _dot(hi, later) + _dot(mid, later) + _dot(lo, later)
        ct_scr[...] = _transpose_exact(c_ref[...])

    c_blk = c_ref[...]
    row = lax.broadcasted_iota(jnp.int32, (ts, ts), 0)
    col = lax.broadcasted_iota(jnp.int32, (ts, ts), 1)
    for j in range(hpg):
        h = g * hpg + j
        cq = _select_lane(c_blk, h)
        d_row = d_scr[pl.ds(h, 1), :]
        cn_row = ct_scr[pl.ds(h, 1), :]
        sl = slice(j * hd, (j + 1) * hd)
        q = q_ref[:, sl]
        s_past = _dg(q, kc_ref[:, j, :].astype(BF16), NT) + (cq + d_row) * LOG2E
        s_new = _dg(q, kn_ref[:, sl], NT) + (cq - cn_row) * LOG2E
        s_new = jnp.where(col <= row, s_new, -jnp.inf)
        m = jnp.maximum(jnp.max(s_past, axis=1, keepdims=True), jnp.max(s_new, axis=1, keepdims=True))
        p_past = jnp.exp2(s_past - m)
        p_new = jnp.exp2(s_new - m)
        den = jnp.sum(p_past, axis=1, keepdims=True) + jnp.sum(p_new, axis=1, keepdims=True)
        o = _dot(p_past.astype(BF16), vc_ref[:, j, :].astype(BF16)) + _dot(p_new.astype(BF16), vn_ref[:, sl])
        o_ref[:, sl] = (o / den).astype(o_ref.dtype)


def _fox_sample(qkv, c, k_cache, v_cache, lf_cache_t, prev_out, *, heads, row0, ts, hpg=8):
    m, width = qkv.shape
    d = width // 3
    hd = d // heads
    n_streams, past = k_cache.shape[:2]
    base = row0 // ts
    ng = heads // hpg
    return pl.pallas_call(
        _fox_sample_kernel, grid=(n_streams, ng),
        in_specs=[pl.BlockSpec((ts, hpg * hd), lambda b, g: (base + b, g)),
                  pl.BlockSpec((ts, hpg * hd), lambda b, g: (base + b, ng + g)),
                  pl.BlockSpec((ts, hpg * hd), lambda b, g: (base + b, 2 * ng + g)),
                  pl.BlockSpec((ts, 128), lambda b, g: (b, 0)),
                  pl.BlockSpec((None, past, hpg, hd), lambda b, g: (b, 0, g, 0)),
                  pl.BlockSpec((None, past, hpg, hd), lambda b, g: (b, 0, g, 0)),
                  pl.BlockSpec((None, heads, past), lambda b, g: (b, 0, 0)),
                  pl.BlockSpec(memory_space=pl.ANY)],
        out_specs=pl.BlockSpec((ts, hpg * hd), lambda b, g: (base + b, g)),
        out_shape=jax.ShapeDtypeStruct((m, d), BF16),
        scratch_shapes=[pltpu.VMEM((heads, past), F32), pltpu.VMEM((128, ts), F32)],
        input_output_aliases={7: 0},
        compiler_params=_params("arbitrary", "arbitrary"), name="fox_sample")(
            qkv, qkv, qkv, c, k_cache, v_cache, lf_cache_t, prev_out)


def _pad_cols(w, n):
    return jnp.pad(w, ((0, 0), (0, n - w.shape[1])))


def _row_vec(v, n=None):
    v = v.astype(F32).reshape(1, -1)
    return v if n is None else _pad_cols(v, n)


def _block_diag(blocks, copies):
    g, p, _ = blocks.shape
    eye = jnp.eye(copies, dtype=blocks.dtype)
    return jnp.einsum("ab,gpq->gapbq", eye, blocks).reshape(g, copies * p, copies * p)


def _gmlp_spatial(w_s, b_s, block, group_dim):
    pos = jnp.arange(block)
    mask = (pos[None, :] // GMLP_CHUNK) <= (pos[:, None] // GMLP_CHUNK)
    ws = jnp.where(mask[None], w_s[:, :block, :block], 0.0)
    copies = ROW_TILE // block
    bias = jnp.repeat(jnp.tile(jnp.transpose(b_s[:, :block]), (copies, 1)), group_dim, axis=1)
    return _block_diag(ws, copies).astype(BF16), bias.astype(F32)


def _cumsum_matrix(size, block):
    r = jnp.arange(size)
    return ((r[None, :] <= r[:, None]) & (r[None, :] // block == r[:, None] // block)).astype(BF16)


def kernel(x_prompt, x_sample, state_b_C, state_b_n, state_b_m, state_c_S, cache_d_k, cache_d_v, cache_d_logf, a_w_in, a_b_in, a_vn_g, a_vn_b, a_w_s, a_b_s, a_w_out, b_w_in, b_b_gates, b_norm_g, b_w_out, c_w_in, c_gn_g, c_gn_b, c_w_out, d_w_in, d_b_f, d_w_out, ffn_w_in, ffn_w_out, ln1_g, ln1_b, ln2_g, ln2_b):
    nbp, seq, d = x_prompt.shape
    nbs, ts, _ = x_sample.shape
    mp, ms = nbp * seq, nbs * ts
    past = cache_d_k.shape[2]
    depth = ffn_w_in.shape[0]
    alpha = (2.0 * depth) ** 0.25
    assert ms == ROW_TILE == SEQ_CHUNK and ROW_TILE % ts == 0 and seq % ATTN_BLOCK == 0 and seq % SEQ_CHUNK == 0 and seq % SUM_TILE == 0
    n_prompt_tiles = mp // ROW_TILE

    x = jnp.concatenate([x_prompt.reshape(mp, d), x_sample.reshape(ms, d)], axis=0)
    ffn_w_in_b, ffn_w_out_b = ffn_w_in.astype(BF16), ffn_w_out.astype(BF16)

    spent = None
    a_vs = []
    b_cp, b_np, b_mp, b_cs, b_ns, b_ms = [], [], [], [], [], []
    c_sp, c_ss = [], []
    d_kp, d_vp, d_fp, d_ks, d_vs, d_fs = [], [], [], [], [], []

    for i in range(depth):
        kind, j = i % 4, i // 4
        if kind == 0:
            w_in = a_w_in[j].astype(BF16)
            u, v, v_last = _gmlp_in(x, w_in[:, :d], _row_vec(a_b_in[j, :d]), w_in[:, d:], _row_vec(a_b_in[j, d:]),
                                    _row_vec(a_vn_g[j]), _row_vec(a_vn_b[j]))
            gd = d // GMLP_GROUPS
            wp, bp = _gmlp_spatial(a_w_s[j], a_b_s[j], GMLP_BLOCK, gd)
            wsm, bsm = _gmlp_spatial(a_w_s[j], a_b_s[j], ts, gd)
            mix = _gmlp_gate(u, v, jnp.stack([wp, wsm]), jnp.stack([bp, bsm]), n_prompt_tiles)
            w_out = a_w_out[j]
            a_vs.append(v_last[v_last.shape[0] - ms:].reshape(nbs, ts, d))
        elif kind == 1:
            heads = MLSTM_HEADS
            n_main = b_w_in.shape[2] - 2 * heads
            dqk = (n_main - 2 * d) // (2 * heads)
            qkv, gates = _qkv_proj(x, b_w_in[j, :, :n_main].astype(BF16),
                                   _pad_cols(b_w_in[j, :, n_main:], 128).astype(BF16), _row_vec(b_b_gates[j], 128),
                                   scaled_cols=heads * dqk, scale=dqk ** -0.5, n_raw=heads, tn=PROJ_WIDE_COLS)
            norm_g = _row_vec(b_norm_g[j])
            zc = jnp.zeros((nbp,) + state_b_C.shape[2:], F32)
            zn = jnp.zeros((nbp,) + state_b_n.shape[2:], F32)
            zm = jnp.zeros((nbp,) + state_b_m.shape[2:], F32)
            mix_s, cs, ns, ms_state = _mlstm(qkv, gates, norm_g, state_b_C[j], state_b_n[j], state_b_m[j],
                                             chunk=ts, row0=mp, seq_len=ts)
            mix, cp, np_, mp_state = _mlstm(qkv, gates, norm_g, zc, zn, zm, chunk=SEQ_CHUNK, row0=0, seq_len=seq,
                                            tail=mix_s)
            b_cp.append(cp); b_np.append(np_); b_mp.append(mp_state)
            b_cs.append(cs); b_ns.append(ns); b_ms.append(ms_state)
            w_out = b_w_out[j]
        elif kind == 2:
            heads = RET_HEADS
            dk = d // heads
            half = dk // 2
            inv = ROPE_BASE ** (-jnp.arange(half, dtype=F32) / half)
            ang_p = jnp.arange(seq, dtype=F32)[:, None] * inv[None, :]
            ang_s = (past + jnp.arange(ts)).astype(F32)[:, None] * inv[None, :]
            rows = lambda f: jnp.concatenate([jnp.tile(f(ang_p), (nbp, 1)), jnp.tile(f(ang_s), (nbs, 1))])
            qkvg = _rot_proj(x, c_w_in[j].astype(BF16), rows(jnp.cos), rows(jnp.sin),
                             n_q=d // PROJ_WIDE_COLS, n_k=d // PROJ_WIDE_COLS, k_scale=dk ** -0.5, head=dk,
                             tn=PROJ_WIDE_COLS)
            gn_g, gn_b = _row_vec(c_gn_g[j]), _row_vec(c_gn_b[j])
            zs = jnp.zeros((nbp,) + state_c_S.shape[2:], F32)
            mix_s, ss = _retention(qkvg, gn_g, gn_b, state_c_S[j], chunk=ts, row0=mp, seq_len=ts)
            mix, sp = _retention(qkvg, gn_g, gn_b, zs, chunk=SEQ_CHUNK, row0=0, seq_len=seq, tail=mix_s)
            c_sp.append(sp); c_ss.append(ss)
            w_out = c_w_out[j]
        else:
            heads = FOX_HEADS
            hd = d // heads
            qkv, logf, k_p, k_s, v_p, v_s = _qkv_proj(
                x, d_w_in[j, :, :3 * d].astype(BF16), _pad_cols(d_w_in[j, :, 3 * d:], 128).astype(BF16),
                _row_vec(d_b_f[j], 128), scaled_cols=d, scale=hd ** -0.5 * LOG2E, n_raw=0, kv_rows=(mp, ms))
            c = _running_sum(logf, _cumsum_matrix(SUM_TILE, SUM_TILE), row0=0, n_tiles=mp // SUM_TILE,
                             tiles_per_seq=seq // SUM_TILE)
            c_s = _running_sum(logf, _cumsum_matrix(ms, ts), row0=mp, n_tiles=1, tiles_per_seq=1)
            out_buf = spent if spent is not None else jnp.zeros((mp + ms, d), BF16)
            mix = _fox_prompt(qkv, c, out_buf, heads=heads, n_seq=nbp, seq_len=seq, tq=ATTN_BLOCK,
                              hps=ATTN_HEADS_PER_STEP, sub=ATTN_KEY_SUB)
            spent = None
            mix = _fox_sample(qkv, c_s, cache_d_k[j], cache_d_v[j],
                              jnp.transpose(cache_d_logf[j], (0, 2, 1)), mix, heads=heads, row0=mp, ts=ts)
            d_kp.append(k_p.reshape(nbp, seq, heads, hd)); d_ks.append(k_s.reshape(nbs, ts, heads, hd))
            d_vp.append(v_p.reshape(nbp, seq, heads, hd)); d_vs.append(v_s.reshape(nbs, ts, heads, hd))
            d_fp.append(logf[:mp, :heads].reshape(nbp, seq, heads))
            d_fs.append(logf[mp:, :heads].reshape(nbs, ts, heads))
            w_out = d_w_out[j]
        x = _out_ln(mix, w_out.astype(BF16), x, _row_vec(ln1_g[i]), _row_vec(ln1_b[i]), alpha)
        if mix.shape == (mp + ms, d):
            spent = mix
        x = _ffn(x, ffn_w_in_b, ffn_w_out_b, i, _row_vec(ln2_g[i]), _row_vec(ln2_b[i]), alpha,
                 final_rows=(mp, ms) if i == depth - 1 else None)

    y_prompt, y_sample = x
    return (y_prompt.reshape(nbp, seq, d), y_sample.reshape(nbs, ts, d), jnp.stack(a_vs),
            jnp.stack(b_cp), jnp.stack(b_np), jnp.stack(b_mp),
            jnp.stack(b_cs), jnp.stack(b_ns), jnp.stack(b_ms),
            jnp.stack(c_sp), jnp.stack(c_ss),
            jnp.stack(d_kp), jnp.stack(d_vp), jnp.stack(d_fp),
            jnp.stack(d_ks), jnp.stack(d_vs), jnp.stack(d_fs))
```
